```python
import math
import jax, jax.numpy as jnp
from jax import lax
import numpy as np

D_MODEL = 1024
BATCH = 16
SEQ = 2048
DEPTH = 1
DEC_BATCH = 8
DEC_SEQ = 2048
PAST_LEN = 128

HEAD_DIM = 64
MIX_WIDTH = D_MODEL
A_WIDTH = MIX_WIDTH // 2
B_WIDTH = MIX_WIDTH - A_WIDTH
A_HEADS = A_WIDTH // HEAD_DIM
A_KV_HEADS = A_HEADS // 4
A_GROUP = A_HEADS // A_KV_HEADS
B_HEADS = B_WIDTH // HEAD_DIM
Q_BLOCK = 128
GRID_W = 64
WIN_ROWS = 8
WIN_COLS = 16
ROPE_THETA = 10000.0
ROPE_AXIS_DIM = HEAD_DIM // 2
D_FF = 4 * D_MODEL
EPS = 1e-6
IN_COLS = A_WIDTH + 2 * A_KV_HEADS * HEAD_DIM + 3 * B_WIDTH

kernel_name = "hybrid_gqa_axialrope_natten_encoder"


def rms_norm(x, g):
    xf = x.astype(jnp.float32)
    y = xf * lax.rsqrt(jnp.mean(xf * xf, axis=-1, keepdims=True) + EPS)
    return (y * g.astype(jnp.float32)).astype(x.dtype)


def rope_1d(x, pos):
    half = x.shape[-1] // 2
    freqs = ROPE_THETA ** (-jnp.arange(half, dtype=jnp.float32) / half)
    ang = pos.astype(jnp.float32)[:, None] * freqs[None, :]
    cos = jnp.cos(ang).astype(x.dtype)
    sin = jnp.sin(ang).astype(x.dtype)
    x1, x2 = x[..., :half], x[..., half:]
    return jnp.concatenate([x1 * cos - x2 * sin, x1 * sin + x2 * cos], axis=-1)


def rope_2d_axial(x, seq_len):
    t = jnp.arange(seq_len, dtype=jnp.int32)
    row, col = t // GRID_W, t % GRID_W
    return jnp.concatenate([rope_1d(x[..., :ROPE_AXIS_DIM], row),
                            rope_1d(x[..., ROPE_AXIS_DIM:], col)], axis=-1)


def global_gqa_attention(q, k, v):
    b, kvh, g, s, dh = q.shape
    nb = s // Q_BLOCK
    scale = dh ** -0.5
    qb = q.reshape(b, kvh, g, nb, Q_BLOCK, dh).transpose(3, 0, 1, 2, 4, 5)

    def one_block(q_blk):
        sc = jnp.einsum('bkgqd,bksd->bkgqs', q_blk, k).astype(jnp.float32) * scale
        p = jax.nn.softmax(sc, axis=-1).astype(v.dtype)
        return jnp.einsum('bkgqs,bksd->bkgqd', p, v)

    o = lax.map(one_block, qb)
    o = o.transpose(1, 0, 4, 2, 3, 5)
    return o.reshape(b, s, kvh * g * dh)


def neighbourhood_tables(rows):
    wr = min(WIN_ROWS, rows)
    s = rows * GRID_W
    t = np.arange(s)
    r, c = t // GRID_W, t % GRID_W
    rs = np.clip(r - wr // 2, 0, rows - wr)
    cs = np.clip(c - WIN_COLS // 2, 0, GRID_W - WIN_COLS)
    kr = rs[:, None, None] + np.arange(wr)[None, :, None]
    kc = cs[:, None, None] + np.arange(WIN_COLS)[None, None, :]
    kr, kc = np.broadcast_arrays(kr, kc)
    n_keys = wr * WIN_COLS
    idx = (kr * GRID_W + kc).reshape(s, n_keys)
    dr = (kr - r[:, None, None] + (WIN_ROWS - 1)).reshape(s, n_keys)
    dc = (kc - c[:, None, None] + (WIN_COLS - 1)).reshape(s, n_keys)
    return (jnp.asarray(idx, jnp.int32), jnp.asarray(dr, jnp.int32),
            jnp.asarray(dc, jnp.int32), n_keys)


def neighbourhood_attention(q, k, v, rel_bias, rows):
    b, hb, s, dh = q.shape
    idx, dr, dc, n_keys = neighbourhood_tables(rows)
    scale = dh ** -0.5
    bias = rel_bias.astype(jnp.float32)[:, dr, dc]
    bias = bias.reshape(hb, rows, GRID_W, n_keys).transpose(1, 0, 2, 3)
    qr = q.reshape(b, hb, rows, GRID_W, dh).transpose(2, 0, 1, 3, 4)
    idx_r = idx.reshape(rows, GRID_W, n_keys)

    def one_row(args):
        q_blk, idx_blk, bias_blk = args
        kg = jnp.take(k, idx_blk, axis=2)
        vg = jnp.take(v, idx_blk, axis=2)
        sc = jnp.einsum('bhqd,bhqkd->bhqk', q_blk, kg).astype(jnp.float32) * scale
        p = jax.nn.softmax(sc + bias_blk[None], axis=-1).astype(v.dtype)
        return jnp.einsum('bhqk,bhqkd->bhqd', p, vg)

    o = lax.map(one_row, (qr, idx_r, bias))
    o = o.transpose(1, 0, 3, 2, 4)
    return o.reshape(b, s, hb * dh)


def hybrid_layer(x, norm_attn_g, w_in, q_norm_g, k_norm_g, nat_rel_bias,
                 out_norm_a_g, out_norm_b_g, w_out, norm_mlp_g, w_up, w_down):
    b, s, _ = x.shape
    rows = s // GRID_W
    h = rms_norm(x, norm_attn_g)
    proj = h @ w_in
    kv_w = A_KV_HEADS * HEAD_DIM
    o1 = A_WIDTH
    o2 = o1 + kv_w
    o3 = o2 + kv_w
    o4 = o3 + B_WIDTH
    o5 = o4 + B_WIDTH
    qa = proj[..., :o1].reshape(b, s, A_KV_HEADS, A_GROUP, HEAD_DIM).transpose(0, 2, 3, 1, 4)
    ka = proj[..., o1:o2].reshape(b, s, A_KV_HEADS, HEAD_DIM).transpose(0, 2, 1, 3)
    va = proj[..., o2:o3].reshape(b, s, A_KV_HEADS, HEAD_DIM).transpose(0, 2, 1, 3)
    qa = rope_2d_axial(rms_norm(qa, q_norm_g), s)
    ka = rope_2d_axial(rms_norm(ka, k_norm_g), s)
    ya = global_gqa_attention(qa, ka, va)
    qb = proj[..., o3:o4].reshape(b, s, B_HEADS, HEAD_DIM).transpose(0, 2, 1, 3)
    kb = proj[..., o4:o5].reshape(b, s, B_HEADS, HEAD_DIM).transpose(0, 2, 1, 3)
    vb = proj[..., o5:].reshape(b, s, B_HEADS, HEAD_DIM).transpose(0, 2, 1, 3)
    yb = neighbourhood_attention(qb, kb, vb, nat_rel_bias, rows)
    y = jnp.concatenate([rms_norm(ya, out_norm_a_g), rms_norm(yb, out_norm_b_g)], axis=-1)
    x = x + y @ w_out
    u = rms_norm(x, norm_mlp_g) @ w_up
    u = jnp.square(jax.nn.relu(u))
    return x + u @ w_down


def setup_inputs(seed: int = 0) -> dict:
    key = jax.random.key(seed)
    ks = jax.random.split(key, 16)
    f32 = jnp.float32

    def gain(k, shape):
        return (1.0 + 0.02 * jax.random.normal(k, shape)).astype(f32)

    def weight(k, shape, fan_in):
        return (jax.random.normal(k, shape) * fan_in ** -0.5).astype(f32)

    return {
        "x_prompt": jax.random.normal(ks[0], (BATCH, SEQ, D_MODEL), f32),
        "x_sample": jax.random.normal(ks[1], (DEC_BATCH, DEC_SEQ, D_MODEL), f32),
        "norm_attn_g": gain(ks[2], (DEPTH, D_MODEL)),
        "w_in": weight(ks[3], (DEPTH, D_MODEL, IN_COLS), D_MODEL),
        "q_norm_g": gain(ks[4], (DEPTH, HEAD_DIM)),
        "k_norm_g": gain(ks[5], (DEPTH, HEAD_DIM)),
        "nat_rel_bias": (0.1 * jax.random.normal(ks[6], (DEPTH, B_HEADS, 2 * WIN_ROWS - 1, 2 * WIN_COLS - 1))).astype(f32),
        "out_norm_a_g": gain(ks[7], (DEPTH, A_WIDTH)),
        "out_norm_b_g": gain(ks[8], (DEPTH, B_WIDTH)),
        "w_out": weight(ks[9], (DEPTH, MIX_WIDTH, D_MODEL), MIX_WIDTH),
        "norm_mlp_g": gain(ks[10], (DEPTH, D_MODEL)),
        "w_up": weight(ks[11], (DEPTH, D_MODEL, D_FF), D_MODEL),
        "w_down": weight(ks[12], (DEPTH, D_FF, D_MODEL), D_FF),
        "final_norm_g": gain(ks[13], (D_MODEL,)),
    }


def encoder(x, norm_attn_g, w_in, q_norm_g, k_norm_g, nat_rel_bias, out_norm_a_g,
            out_norm_b_g, w_out, norm_mlp_g, w_up, w_down, final_norm_g):
    for l in range(DEPTH):
        x = hybrid_layer(x, norm_attn_g[l], w_in[l], q_norm_g[l], k_norm_g[l],
                         nat_rel_bias[l], out_norm_a_g[l], out_norm_b_g[l], w_out[l],
                         norm_mlp_g[l], w_up[l], w_down[l])
    return rms_norm(x, final_norm_g)


def reference(x_prompt, x_sample, norm_attn_g, w_in, q_norm_g, k_norm_g, nat_rel_bias,
              out_norm_a_g, out_norm_b_g, w_out, norm_mlp_g, w_up, w_down, final_norm_g):
    y_prompt = encoder(x_prompt, norm_attn_g, w_in, q_norm_g, k_norm_g, nat_rel_bias,
                       out_norm_a_g, out_norm_b_g, w_out, norm_mlp_g, w_up, w_down,
                       final_norm_g)
    y_sample = encoder(x_sample, norm_attn_g, w_in, q_norm_g, k_norm_g, nat_rel_bias,
                       out_norm_a_g, out_norm_b_g, w_out, norm_mlp_g, w_up, w_down,
                       final_norm_g)
    return (y_prompt, y_sample)
```

```python
import functools

import jax
import jax.numpy as jnp
from jax import lax
from jax.experimental import pallas as pl
from jax.experimental.pallas import tpu as pltpu

D_MODEL = 1024
HEAD_DIM = 64
LANES = 128
A_WIDTH = 512
B_WIDTH = 512
A_KV_HEADS = 2
A_GROUP = 4
B_HEADS = 8
GRID_W = 64
WIN_ROWS = 8
WIN_COLS = 16
ROPE_THETA = 10000.0
ROPE_HALF = 16
D_FF = 4 * D_MODEL
EPS = 1e-6
SCALE = HEAD_DIM ** -0.5
MASKED = -1e30

N_PAIRS = A_WIDTH // LANES
BAND = WIN_ROWS * GRID_W
KV_DUP = 2 * A_KV_HEADS * LANES
PROJ_TM = 512
ATTN_TQ = 1024
MLP_TM = 512
FF_CHUNK = 1024
VMEM_LIMIT = 56 * 1024 * 1024

_NT = (((1,), (1,)), ((), ()))


def _const_spec(shape):
    zeros = (0,) * len(shape)
    return pl.BlockSpec(shape, lambda *_: zeros, pipeline_mode=pl.Buffered(1))


def _rms(x, g):
    return x * lax.rsqrt(jnp.mean(x * x, axis=-1, keepdims=True) + EPS) * g


def _bias_kernel(rb_ref, out_ref):
    h = pl.program_id(0)
    n_dr = 2 * WIN_ROWS - 1
    n_dc = 2 * WIN_COLS - 1
    c = lax.broadcasted_iota(jnp.int32, (GRID_W, LANES), 0)
    lane = lax.broadcasted_iota(jnp.int32, (GRID_W, LANES), 1)
    kc = lane % GRID_W
    first = lane < GRID_W
    d = kc - c + (WIN_COLS - 1)
    cs = jnp.clip(c - WIN_COLS // 2, 0, GRID_W - WIN_COLS)
    valid = (kc >= cs) & (kc < cs + WIN_COLS)
    hits = [d == dd for dd in range(n_dc)]
    base = h * (n_dr * n_dc)
    pair_tiles = []
    for dr in range(n_dr - 1):
        acc = jnp.zeros((GRID_W, LANES), jnp.float32)
        for dd in range(n_dc):
            v0 = rb_ref[base + dr * n_dc + dd]
            v1 = rb_ref[base + (dr + 1) * n_dc + dd]
            acc = jnp.where(hits[dd], jnp.where(first, v0, v1), acc)
        pair_tiles.append(jnp.where(valid, acc, MASKED))
    for a in range(WIN_ROWS):
        for jj in range(WIN_ROWS // 2):
            out_ref[0, a, :, jj * LANES:(jj + 1) * LANES] = pair_tiles[a + 2 * jj]


def _bias_tables(rel_bias):
    flat = rel_bias.reshape(-1).astype(jnp.float32)
    return pl.pallas_call(
        _bias_kernel,
        grid=(B_HEADS,),
        in_specs=[pl.BlockSpec(memory_space=pltpu.SMEM)],
        out_specs=pl.BlockSpec((1, WIN_ROWS, GRID_W, BAND), lambda h: (h, 0, 0, 0)),
        out_shape=jax.ShapeDtypeStruct((B_HEADS, WIN_ROWS, GRID_W, BAND), jnp.float32),
        name="nat_bias_tables",
    )(flat)


def _proj_kernel(x_ref, g_ref, w_ref, qg_ref, kg_ref, cos_ref, sa_ref, sb_ref,
                 qa_ref, ka_ref, va_ref, qb_ref, kb_ref, vb_ref):
    x = x_ref[0]
    h = _rms(x, g_ref[...]).astype(jnp.bfloat16)
    lo = lax.broadcasted_iota(jnp.int32, (1, LANES), 1) < HEAD_DIM
    cos = cos_ref[...]
    sa = sa_ref[...]
    sb = sb_ref[...]

    def proj(col, width):
        return jnp.dot(h, w_ref[:, col:col + width], preferred_element_type=jnp.float32)

    def norm_rope(t, g):
        sq = t * t
        s_lo = jnp.sum(jnp.where(lo, sq, 0.0), axis=-1, keepdims=True)
        s_hi = jnp.sum(jnp.where(lo, 0.0, sq), axis=-1, keepdims=True)
        ms = jnp.where(lo, s_lo, s_hi) * (1.0 / HEAD_DIM)
        y = t * lax.rsqrt(ms + EPS) * g
        return (y * cos + pltpu.roll(y, LANES - ROPE_HALF, 1) * sa
                + pltpu.roll(y, ROPE_HALF, 1) * sb)

    qa = proj(0, A_WIDTH)
    for p in range(N_PAIRS):
        t = norm_rope(qa[:, p * LANES:(p + 1) * LANES], qg_ref[...])
        qa_ref[0, p] = (t * SCALE).astype(jnp.bfloat16)
    ka = proj(A_WIDTH, A_KV_HEADS * LANES)
    for p in range(A_KV_HEADS):
        t = norm_rope(ka[:, p * LANES:(p + 1) * LANES], kg_ref[...])
        ka_ref[0, p] = t.astype(jnp.bfloat16)
    va = proj(A_WIDTH + A_KV_HEADS * LANES, A_KV_HEADS * LANES)
    for p in range(A_KV_HEADS):
        va_ref[0, p] = va[:, p * LANES:(p + 1) * LANES].astype(jnp.bfloat16)
    col = A_WIDTH + KV_DUP
    qb = proj(col, B_WIDTH)
    kb = proj(col + B_WIDTH, B_WIDTH)
    vb = proj(col + 2 * B_WIDTH, B_WIDTH)
    for p in range(N_PAIRS):
        sl = slice(p * LANES, (p + 1) * LANES)
        qb_ref[0, p] = (qb[:, sl] * SCALE).astype(jnp.bfloat16)
        kb_ref[0, p] = kb[:, sl].astype(jnp.bfloat16)
        vb_ref[0, p] = vb[:, sl].astype(jnp.bfloat16)


def _in_projection(x, norm_g, w_ext, q_g, k_g, cos, sa, sb):
    b, s, d = x.shape
    tm = PROJ_TM
    n_cols = w_ext.shape[1]
    pair = lambda n: jax.ShapeDtypeStruct((b, n, s, LANES), jnp.bfloat16)
    pair_spec = lambda n: pl.BlockSpec((1, n, tm, LANES), lambda bi, si: (bi, 0, si, 0))
    tab_spec = pl.BlockSpec((tm, LANES), lambda bi, si: (si, 0))
    return pl.pallas_call(
        _proj_kernel,
        grid=(b, s // tm),
        in_specs=[
            pl.BlockSpec((1, tm, d), lambda bi, si: (bi, si, 0)),
            _const_spec((1, d)),
            _const_spec((d, n_cols)),
            _const_spec((1, LANES)),
            _const_spec((1, LANES)),
            tab_spec, tab_spec, tab_spec,
        ],
        out_specs=[pair_spec(N_PAIRS), pair_spec(A_KV_HEADS), pair_spec(A_KV_HEADS),
                   pair_spec(N_PAIRS), pair_spec(N_PAIRS), pair_spec(N_PAIRS)],
        out_shape=[pair(N_PAIRS), pair(A_KV_HEADS), pair(A_KV_HEADS),
                   pair(N_PAIRS), pair(N_PAIRS), pair(N_PAIRS)],
        compiler_params=pltpu.CompilerParams(
            dimension_semantics=("parallel", "parallel"), vmem_limit_bytes=VMEM_LIMIT),
        name="in_projection",
    )(x, norm_g, w_ext, q_g, k_g, cos, sa, sb)


def _softmax_pv(s, v):
    m = jnp.max(s, axis=-1, keepdims=True)
    p = jnp.exp(s - m)
    l = jnp.sum(p, axis=-1, keepdims=True)
    o = jnp.dot(p.astype(jnp.bfloat16), v, preferred_element_type=jnp.float32)
    return o / l


def _global_attn_kernel(q_ref, k_ref, v_ref, o_ref):
    q = q_ref[0, 0]
    k = k_ref[0, 0]
    v = v_ref[0, 0]
    lo = lax.broadcasted_iota(jnp.int32, (1, LANES), 1) < HEAD_DIM
    zero = jnp.zeros_like(q)

    def head(qm):
        s = lax.dot_general(qm, k, _NT, preferred_element_type=jnp.float32)
        return _softmax_pv(s, v)

    o_lo = head(jnp.where(lo, q, zero))
    o_hi = head(jnp.where(lo, zero, q))
    o_ref[0, 0] = jnp.where(lo, o_lo, o_hi).astype(o_ref.dtype)


def _global_attention(qa, ka, va):
    b, n_pairs, s, _ = qa.shape
    tq = ATTN_TQ
    pairs_per_kv = A_GROUP // 2
    kv_spec = pl.BlockSpec((1, 1, s, LANES), lambda bi, p, i: (bi, p // pairs_per_kv, 0, 0))
    q_spec = pl.BlockSpec((1, 1, tq, LANES), lambda bi, p, i: (bi, p, i, 0))
    return pl.pallas_call(
        _global_attn_kernel,
        grid=(b, n_pairs, s // tq),
        in_specs=[q_spec, kv_spec, kv_spec],
        out_specs=q_spec,
        out_shape=jax.ShapeDtypeStruct(qa.shape, jnp.bfloat16),
        compiler_params=pltpu.CompilerParams(
            dimension_semantics=("parallel", "parallel", "parallel"),
            vmem_limit_bytes=VMEM_LIMIT),
        name="global_attention",
    )(qa, ka, va)


def _nat_kernel(q_ref, k_ref, v_ref, bias_ref, o_ref, *, rows):
    lo = lax.broadcasted_iota(jnp.int32, (1, LANES), 1) < HEAD_DIM

    def row_body(r, carry):
        rs = jnp.clip(r - WIN_ROWS // 2, 0, rows - WIN_ROWS)
        a = rs - r + (WIN_ROWS - 1)
        q = q_ref[0, 0, pl.ds(pl.multiple_of(r * GRID_W, GRID_W), GRID_W), :]
        band = pl.ds(pl.multiple_of(rs * GRID_W, GRID_W), BAND)
        k = k_ref[0, 0, band, :]
        v = v_ref[0, 0, band, :]
        zero = jnp.zeros_like(q)

        def head(qm, hh):
            s = lax.dot_general(qm, k, _NT, preferred_element_type=jnp.float32)
            return _softmax_pv(s + bias_ref[hh, a], v)

        o_lo = head(jnp.where(lo, q, zero), 0)
        o_hi = head(jnp.where(lo, zero, q), 1)
        o_ref[0, 0, pl.ds(pl.multiple_of(r * GRID_W, GRID_W), GRID_W), :] = (
            jnp.where(lo, o_lo, o_hi).astype(o_ref.dtype))
        return carry

    lax.fori_loop(0, rows, row_body, 0)


def _neighbourhood_attention(qb, kb, vb, bias):
    b, n_pairs, s, _ = qb.shape
    rows = s // GRID_W
    seq_spec = pl.BlockSpec((1, 1, s, LANES), lambda p, bi: (bi, p, 0, 0))
    bias_spec = pl.BlockSpec((2, WIN_ROWS, GRID_W, BAND), lambda p, bi: (p, 0, 0, 0))
    return pl.pallas_call(
        functools.partial(_nat_kernel, rows=rows),
        grid=(n_pairs, b),
        in_specs=[seq_spec, seq_spec, seq_spec, bias_spec],
        out_specs=seq_spec,
        out_shape=jax.ShapeDtypeStruct(qb.shape, jnp.bfloat16),
        compiler_params=pltpu.CompilerParams(
            dimension_semantics=("parallel", "parallel"), vmem_limit_bytes=VMEM_LIMIT),
        name="neighbourhood_attention",
    )(qb, kb, vb, bias)


def _mlp_kernel(x_ref, ya_ref, yb_ref, ga_ref, gb_ref, wo_ref, gm_ref, wu_ref, wd_ref,
                gf_ref, o_ref):
    ya = jnp.concatenate([ya_ref[0, p] for p in range(N_PAIRS)], axis=-1)
    yb = jnp.concatenate([yb_ref[0, p] for p in range(N_PAIRS)], axis=-1)
    y = jnp.concatenate([_rms(ya.astype(jnp.float32), ga_ref[...]),
                         _rms(yb.astype(jnp.float32), gb_ref[...])], axis=-1)
    x1 = x_ref[0] + jnp.dot(y.astype(jnp.bfloat16), wo_ref[...],
                            preferred_element_type=jnp.float32)
    h = _rms(x1, gm_ref[...]).astype(jnp.bfloat16)
    acc = x1
    for c in range(D_FF // FF_CHUNK):
        sl = slice(c * FF_CHUNK, (c + 1) * FF_CHUNK)
        u = jnp.dot(h, wu_ref[:, sl], preferred_element_type=jnp.float32)
        u = jnp.square(jnp.maximum(u, 0.0)).astype(jnp.bfloat16)
        acc = acc + jnp.dot(u, wd_ref[sl, :], preferred_element_type=jnp.float32)
    o_ref[0] = _rms(acc, gf_ref[...])


def _out_mlp(x, ya, yb, ga, gb, w_out, gm, w_up, w_down, gf):
    b, s, d = x.shape
    tm = MLP_TM
    x_spec = pl.BlockSpec((1, tm, d), lambda bi, si: (bi, si, 0))
    y_spec = pl.BlockSpec((1, N_PAIRS, tm, LANES), lambda bi, si: (bi, 0, si, 0))
    return pl.pallas_call(
        _mlp_kernel,
        grid=(b, s // tm),
        in_specs=[x_spec, y_spec, y_spec,
                  _const_spec((1, A_WIDTH)), _const_spec((1, B_WIDTH)),
                  _const_spec(w_out.shape), _const_spec((1, d)),
                  _const_spec(w_up.shape), _const_spec(w_down.shape),
                  _const_spec((1, d))],
        out_specs=x_spec,
        out_shape=jax.ShapeDtypeStruct(x.shape, jnp.float32),
        compiler_params=pltpu.CompilerParams(
            dimension_semantics=("parallel", "parallel"), vmem_limit_bytes=VMEM_LIMIT),
        name="out_proj_mlp",
    )(x, ya, yb, ga, gb, w_out, gm, w_up, w_down, gf)


def _rope_tables(seq_len):
    t = jnp.arange(seq_len, dtype=jnp.int32)
    freqs = ROPE_THETA ** (-jnp.arange(ROPE_HALF, dtype=jnp.float32) / ROPE_HALF)
    zeros = jnp.zeros((seq_len, ROPE_HALF), jnp.float32)
    cos_parts, sa_parts, sb_parts = [], [], []
    for pos in (t // GRID_W, t % GRID_W):
        ang = pos.astype(jnp.float32)[:, None] * freqs[None, :]
        c, sn = jnp.cos(ang), jnp.sin(ang)
        cos_parts += [c, c]
        sa_parts += [-sn, zeros]
        sb_parts += [zeros, sn]
    head = lambda parts: jnp.tile(jnp.concatenate(parts, axis=-1), (1, LANES // HEAD_DIM))
    return head(cos_parts), head(sa_parts), head(sb_parts)


def _extended_w_in(w_in):
    o1 = A_WIDTH
    kv_w = A_KV_HEADS * HEAD_DIM
    cols = [w_in[:, :o1]]
    for base in (o1, o1 + kv_w):
        for hd in range(A_KV_HEADS):
            w = w_in[:, base + hd * HEAD_DIM: base + (hd + 1) * HEAD_DIM]
            cols += [w, w]
    cols.append(w_in[:, o1 + 2 * kv_w:])
    return jnp.concatenate(cols, axis=1).astype(jnp.bfloat16)


def _encoder(x, p):
    qa, ka, va, qb, kb, vb = _in_projection(
        x, p["norm_attn_g"], p["w_in"], p["q_g"], p["k_g"], *p["rope"])
    ya = _global_attention(qa, ka, va)
    yb = _neighbourhood_attention(qb, kb, vb, p["bias"])
    return _out_mlp(x, ya, yb, p["ga"], p["gb"], p["w_out"], p["gm"], p["w_up"],
                    p["w_down"], p["gf"])


def kernel(x_prompt, x_sample, norm_attn_g, w_in, q_norm_g, k_norm_g, nat_rel_bias,
           out_norm_a_g, out_norm_b_g, w_out, norm_mlp_g, w_up, w_down, final_norm_g):
    assert norm_attn_g.shape[0] == 1, "single trunk layer"
    assert x_prompt.shape[1] == x_sample.shape[1]
    row = lambda g: g.reshape(1, -1).astype(jnp.float32)
    pair_gain = lambda g: jnp.tile(row(g), (1, LANES // HEAD_DIM))
    params = {
        "norm_attn_g": row(norm_attn_g[0]),
        "w_in": _extended_w_in(w_in[0]),
        "q_g": pair_gain(q_norm_g[0]),
        "k_g": pair_gain(k_norm_g[0]),
        "rope": _rope_tables(x_prompt.shape[1]),
        "bias": _bias_tables(nat_rel_bias[0]),
        "ga": row(out_norm_a_g[0]),
        "gb": row(out_norm_b_g[0]),
        "w_out": w_out[0].astype(jnp.bfloat16),
        "gm": row(norm_mlp_g[0]),
        "w_up": w_up[0].astype(jnp.bfloat16),
        "w_down": w_down[0].astype(jnp.bfloat16),
        "gf": row(final_norm_g),
    }
    return (_encoder(x_prompt, params), _encoder(x_sample, params))
```

```python
import functools

import jax
import jax.numpy as jnp
from jax import lax
from jax.experimental import pallas as pl
from jax.experimental.pallas import tpu as pltpu

D_MODEL = 1024
HEAD_DIM = 64
LANES = 128
A_WIDTH = 512
B_WIDTH = 512
A_KV_HEADS = 2
A_GROUP = 4
B_HEADS = 8
GRID_W = 64
WIN_ROWS = 8
WIN_COLS = 16
ROPE_THETA = 10000.0
ROPE_HALF = 16
D_FF = 4 * D_MODEL
EPS = 1e-6
SCALE = HEAD_DIM ** -0.5
MASKED = -1e30

N_PAIRS = A_WIDTH // LANES
BAND = WIN_ROWS * GRID_W
KV_DUP = 2 * A_KV_HEADS * LANES
PROJ_TM = 512
ATTN_TQ = 1024
SOFTMAX_ROWS = 16
LOG2E = 1.4426950408889634
MLP_TM = 512
FF_CHUNK = 1024
VMEM_LIMIT = 56 * 1024 * 1024

_NT = (((1,), (1,)), ((), ()))


def _const_spec(shape):
    zeros = (0,) * len(shape)
    return pl.BlockSpec(shape, lambda *_: zeros, pipeline_mode=pl.Buffered(1))


def _rms(x, g):
    return x * lax.rsqrt(jnp.mean(x * x, axis=-1, keepdims=True) + EPS) * g


def _bias_kernel(rb_ref, out_ref):
    h = pl.program_id(0)
    n_dr = 2 * WIN_ROWS - 1
    n_dc = 2 * WIN_COLS - 1
    c = lax.broadcasted_iota(jnp.int32, (GRID_W, LANES), 0)
    lane = lax.broadcasted_iota(jnp.int32, (GRID_W, LANES), 1)
    kc = lane % GRID_W
    first = lane < GRID_W
    d = kc - c + (WIN_COLS - 1)
    cs = jnp.clip(c - WIN_COLS // 2, 0, GRID_W - WIN_COLS)
    valid = (kc >= cs) & (kc < cs + WIN_COLS)
    hits = [d == dd for dd in range(n_dc)]
    base = h * (n_dr * n_dc)
    pair_tiles = []
    for dr in range(n_dr - 1):
        acc = jnp.zeros((GRID_W, LANES), jnp.float32)
        for dd in range(n_dc):
            v0 = rb_ref[base + dr * n_dc + dd]
            v1 = rb_ref[base + (dr + 1) * n_dc + dd]
            acc = jnp.where(hits[dd], jnp.where(first, v0, v1), acc)
        pair_tiles.append(jnp.where(valid, acc * LOG2E, MASKED))
    for a in range(WIN_ROWS):
        for jj in range(WIN_ROWS // 2):
            out_ref[0, a, :, jj * LANES:(jj + 1) * LANES] = pair_tiles[a + 2 * jj]


def _bias_tables(rel_bias):
    flat = rel_bias.reshape(-1).astype(jnp.float32)
    return pl.pallas_call(
        _bias_kernel,
        grid=(B_HEADS,),
        in_specs=[pl.BlockSpec(memory_space=pltpu.SMEM)],
        out_specs=pl.BlockSpec((1, WIN_ROWS, GRID_W, BAND), lambda h: (h // 2, 0, h % 2, 0)),
        out_shape=jax.ShapeDtypeStruct((N_PAIRS, WIN_ROWS, 2 * GRID_W, BAND), jnp.float32),
        name="nat_bias_tables",
    )(flat)


def _proj_kernel(x_ref, g_ref, w_ref, qg_ref, kg_ref, cos_ref, sa_ref, sb_ref,
                 qa_ref, ka_ref, va_ref, qb_ref, kb_ref, vb_ref):
    x = x_ref[0]
    h = _rms(x, g_ref[...]).astype(jnp.bfloat16)
    lo = lax.broadcasted_iota(jnp.int32, (1, LANES), 1) < HEAD_DIM
    cos = cos_ref[...]
    sa = sa_ref[...]
    sb = sb_ref[...]

    def proj(col, width):
        return jnp.dot(h, w_ref[:, col:col + width], preferred_element_type=jnp.float32)

    def norm_rope(t, g):
        sq = t * t
        s_lo = jnp.sum(jnp.where(lo, sq, 0.0), axis=-1, keepdims=True)
        s_hi = jnp.sum(jnp.where(lo, 0.0, sq), axis=-1, keepdims=True)
        ms = jnp.where(lo, s_lo, s_hi) * (1.0 / HEAD_DIM)
        y = t * lax.rsqrt(ms + EPS) * g
        return (y * cos + pltpu.roll(y, LANES - ROPE_HALF, 1) * sa
                + pltpu.roll(y, ROPE_HALF, 1) * sb)

    qa = proj(0, A_WIDTH)
    for p in range(N_PAIRS):
        t = norm_rope(qa[:, p * LANES:(p + 1) * LANES], qg_ref[...])
        qa_ref[0, p] = (t * (SCALE * LOG2E)).astype(jnp.bfloat16)
    ka = proj(A_WIDTH, A_KV_HEADS * LANES)
    for p in range(A_KV_HEADS):
        t = norm_rope(ka[:, p * LANES:(p + 1) * LANES], kg_ref[...])
        ka_ref[0, p] = t.astype(jnp.bfloat16)
    va = proj(A_WIDTH + A_KV_HEADS * LANES, A_KV_HEADS * LANES)
    for p in range(A_KV_HEADS):
        va_ref[0, p] = va[:, p * LANES:(p + 1) * LANES].astype(jnp.bfloat16)
    col = A_WIDTH + KV_DUP
    qb = proj(col, B_WIDTH)
    kb = proj(col + B_WIDTH, B_WIDTH)
    vb = proj(col + 2 * B_WIDTH, B_WIDTH)
    for p in range(N_PAIRS):
        sl = slice(p * LANES, (p + 1) * LANES)
        qb_ref[0, p] = (qb[:, sl] * (SCALE * LOG2E)).astype(jnp.bfloat16)
        kb_ref[0, p] = kb[:, sl].astype(jnp.bfloat16)
        vb_ref[0, p] = vb[:, sl].astype(jnp.bfloat16)


def _in_projection(x, norm_g, w_ext, q_g, k_g, cos, sa, sb):
    b, s, d = x.shape
    tm = PROJ_TM
    n_cols = w_ext.shape[1]
    pair = lambda n: jax.ShapeDtypeStruct((b, n, s, LANES), jnp.bfloat16)
    pair_spec = lambda n: pl.BlockSpec((1, n, tm, LANES), lambda bi, si: (bi, 0, si, 0))
    tab_spec = pl.BlockSpec((tm, LANES), lambda bi, si: (si, 0))
    return pl.pallas_call(
        _proj_kernel,
        grid=(b, s // tm),
        in_specs=[
            pl.BlockSpec((1, tm, d), lambda bi, si: (bi, si, 0)),
            _const_spec((1, d)),
            _const_spec((d, n_cols)),
            _const_spec((1, LANES)),
            _const_spec((1, LANES)),
            tab_spec, tab_spec, tab_spec,
        ],
        out_specs=[pair_spec(N_PAIRS), pair_spec(A_KV_HEADS), pair_spec(A_KV_HEADS),
                   pair_spec(N_PAIRS), pair_spec(N_PAIRS), pair_spec(N_PAIRS)],
        out_shape=[pair(N_PAIRS), pair(A_KV_HEADS), pair(A_KV_HEADS),
                   pair(N_PAIRS), pair(N_PAIRS), pair(N_PAIRS)],
        compiler_params=pltpu.CompilerParams(
            dimension_semantics=("parallel", "parallel"), vmem_limit_bytes=VMEM_LIMIT),
        name="in_projection",
    )(x, norm_g, w_ext, q_g, k_g, cos, sa, sb)


def _exp2_rows(s_ref, m_ref, r0, r1):
    blocks = []
    for a in range(r0, r1, SOFTMAX_ROWS):
        rows = slice(a, a + SOFTMAX_ROWS)
        s = s_ref[rows]
        m = jnp.tile(m_ref[rows], (1, s.shape[1] // LANES))
        blocks.append(jnp.exp2(s - m).astype(jnp.bfloat16))
    return jnp.concatenate(blocks, axis=0)


def _row_max(s_ref, m_ref, r0, r1):
    for a in range(r0, r1, SOFTMAX_ROWS):
        rows = slice(a, a + SOFTMAX_ROWS)
        m = jnp.max(s_ref[rows], axis=-1, keepdims=True)
        m_ref[rows] = jnp.broadcast_to(m, (SOFTMAX_ROWS, LANES))


def _with_ones(v):
    return jnp.concatenate([v, jnp.ones_like(v)], axis=1)


def _normalise(res):
    return res[:, :LANES] / res[:, LANES:]


def _global_attn_kernel(q_ref, k_ref, v_ref, o_ref, s_a, s_b, m_a, m_b):
    g = pl.program_id(0)
    tq = q_ref.shape[2]
    lo = lax.broadcasted_iota(jnp.int32, (1, LANES), 1) < HEAD_DIM

    @pl.when(g == 0)
    def _():
        s_b[...] = jnp.zeros_like(s_b)
        m_b[...] = jnp.zeros_like(m_b)

    def step(s_w, m_w, s_r, m_r):
        q = q_ref[0, 0]
        k = k_ref[0, 0]
        zero = jnp.zeros_like(q)
        s_w[0:tq] = lax.dot_general(jnp.where(lo, q, zero), k, _NT,
                                    preferred_element_type=jnp.float32)
        s_w[tq:2 * tq] = lax.dot_general(jnp.where(lo, zero, q), k, _NT,
                                         preferred_element_type=jnp.float32)
        _row_max(s_w, m_w, 0, 2 * tq)

        v1 = _with_ones(v_ref[0, 0])
        o_lo = _normalise(jnp.dot(_exp2_rows(s_r, m_r, 0, tq), v1,
                                  preferred_element_type=jnp.float32))
        o_hi = _normalise(jnp.dot(_exp2_rows(s_r, m_r, tq, 2 * tq), v1,
                                  preferred_element_type=jnp.float32))
        o_ref[0, 0] = jnp.where(lo, o_lo, o_hi).astype(o_ref.dtype)

    @pl.when(g % 2 == 0)
    def _():
        step(s_a, m_a, s_b, m_b)

    @pl.when(g % 2 == 1)
    def _():
        step(s_b, m_b, s_a, m_a)


def _global_attention(qa, ka, va):
    b, n_pairs, s, _ = qa.shape
    tq = ATTN_TQ
    n_chunks = s // tq
    n_items = b * n_pairs * n_chunks
    pairs_per_kv = A_GROUP // 2

    def item(g, lag):
        i = jnp.clip(g - lag, 0, n_items - 1)
        return i // (n_pairs * n_chunks), (i // n_chunks) % n_pairs, i % n_chunks

    def q_map(lag):
        def index(g):
            bi, p, c = item(g, lag)
            return bi, p, c, 0
        return index

    def kv_map(lag):
        def index(g):
            bi, p, _ = item(g, lag)
            return bi, p // pairs_per_kv, 0, 0
        return index

    return pl.pallas_call(
        _global_attn_kernel,
        grid=(n_items + 1,),
        in_specs=[pl.BlockSpec((1, 1, tq, LANES), q_map(0)),
                  pl.BlockSpec((1, 1, s, LANES), kv_map(0)),
                  pl.BlockSpec((1, 1, s, LANES), kv_map(1))],
        out_specs=pl.BlockSpec((1, 1, tq, LANES), q_map(1)),
        out_shape=jax.ShapeDtypeStruct(qa.shape, jnp.bfloat16),
        scratch_shapes=[pltpu.VMEM((2 * tq, s), jnp.float32),
                        pltpu.VMEM((2 * tq, s), jnp.float32),
                        pltpu.VMEM((2 * tq, LANES), jnp.float32),
                        pltpu.VMEM((2 * tq, LANES), jnp.float32)],
        compiler_params=pltpu.CompilerParams(
            dimension_semantics=("arbitrary",), vmem_limit_bytes=VMEM_LIMIT),
        name="global_attention",
    )(qa, ka, va)


def _nat_kernel(q_ref, k_ref, v_ref, bias_ref, o_ref, s_a, s_b, m_a, m_b, *, rows):
    g = pl.program_id(0)
    lo = lax.broadcasted_iota(jnp.int32, (1, LANES), 1) < HEAD_DIM
    stacked = 2 * GRID_W

    @pl.when(g == 0)
    def _():
        s_b[...] = jnp.zeros_like(s_b)
        m_b[...] = jnp.zeros_like(m_b)

    def band_start(r):
        return min(max(r - WIN_ROWS // 2, 0), rows - WIN_ROWS)

    def step(s_w, m_w, s_r, m_r):
        q = q_ref[0, 0]
        zero = jnp.zeros_like(q)
        q_lo = jnp.where(lo, q, zero)
        q_hi = jnp.where(lo, zero, q)
        for r in range(rows):
            rs = band_start(r)
            tok = slice(r * GRID_W, (r + 1) * GRID_W)
            qs = jnp.concatenate([q_lo[tok], q_hi[tok]], axis=0)
            k = k_ref[0, 0, rs * GRID_W:rs * GRID_W + BAND, :]
            s = lax.dot_general(qs, k, _NT, preferred_element_type=jnp.float32)
            s_w[r * stacked:(r + 1) * stacked] = s + bias_ref[0, rs - r + WIN_ROWS - 1]
        _row_max(s_w, m_w, 0, rows * stacked)

        v1 = _with_ones(v_ref[0, 0])
        for r in range(rows):
            rs = band_start(r)
            p = _exp2_rows(s_r, m_r, r * stacked, (r + 1) * stacked)
            o = _normalise(jnp.dot(p, v1[rs * GRID_W:rs * GRID_W + BAND],
                                   preferred_element_type=jnp.float32))
            o_ref[0, 0, r * GRID_W:(r + 1) * GRID_W, :] = jnp.where(
                lo, o[:GRID_W], o[GRID_W:]).astype(o_ref.dtype)

    @pl.when(g % 2 == 0)
    def _():
        step(s_a, m_a, s_b, m_b)

    @pl.when(g % 2 == 1)
    def _():
        step(s_b, m_b, s_a, m_a)


def _neighbourhood_attention(qb, kb, vb, bias):
    b, n_pairs, s, _ = qb.shape
    rows = s // GRID_W
    n_items = n_pairs * b

    def seq_map(lag):
        def index(g):
            i = jnp.clip(g - lag, 0, n_items - 1)
            return i % b, i // b, 0, 0
        return index

    seq = lambda lag: pl.BlockSpec((1, 1, s, LANES), seq_map(lag))
    bias_spec = pl.BlockSpec((1, WIN_ROWS, 2 * GRID_W, BAND),
                             lambda g: (jnp.minimum(g, n_items - 1) // b, 0, 0, 0))
    score_rows = rows * 2 * GRID_W
    return pl.pallas_call(
        functools.partial(_nat_kernel, rows=rows),
        grid=(n_items + 1,),
        in_specs=[seq(0), seq(0), seq(1), bias_spec],
        out_specs=seq(1),
        out_shape=jax.ShapeDtypeStruct(qb.shape, jnp.bfloat16),
        scratch_shapes=[pltpu.VMEM((score_rows, BAND), jnp.float32),
                        pltpu.VMEM((score_rows, BAND), jnp.float32),
                        pltpu.VMEM((score_rows, LANES), jnp.float32),
                        pltpu.VMEM((score_rows, LANES), jnp.float32)],
        compiler_params=pltpu.CompilerParams(
            dimension_semantics=("arbitrary",), vmem_limit_bytes=VMEM_LIMIT),
        name="neighbourhood_attention",
    )(qb, kb, vb, bias)


def _mlp_kernel(x_ref, ya_ref, yb_ref, ga_ref, gb_ref, wo_ref, gm_ref, wu_ref, wd_ref,
                gf_ref, o_ref):
    ya = jnp.concatenate([ya_ref[0, p] for p in range(N_PAIRS)], axis=-1)
    yb = jnp.concatenate([yb_ref[0, p] for p in range(N_PAIRS)], axis=-1)
    y = jnp.concatenate([_rms(ya.astype(jnp.float32), ga_ref[...]),
                         _rms(yb.astype(jnp.float32), gb_ref[...])], axis=-1)
    x1 = x_ref[0] + jnp.dot(y.astype(jnp.bfloat16), wo_ref[...],
                            preferred_element_type=jnp.float32)
    h = _rms(x1, gm_ref[...]).astype(jnp.bfloat16)
    acc = x1
    for c in range(D_FF // FF_CHUNK):
        sl = slice(c * FF_CHUNK, (c + 1) * FF_CHUNK)
        u = jnp.dot(h, wu_ref[:, sl], preferred_element_type=jnp.float32)
        u = jnp.square(jnp.maximum(u, 0.0)).astype(jnp.bfloat16)
        acc = acc + jnp.dot(u, wd_ref[sl, :], preferred_element_type=jnp.float32)
    o_ref[0] = _rms(acc, gf_ref[...])


def _out_mlp(x, ya, yb, ga, gb, w_out, gm, w_up, w_down, gf):
    b, s, d = x.shape
    tm = MLP_TM
    x_spec = pl.BlockSpec((1, tm, d), lambda bi, si: (bi, si, 0))
    y_spec = pl.BlockSpec((1, N_PAIRS, tm, LANES), lambda bi, si: (bi, 0, si, 0))
    return pl.pallas_call(
        _mlp_kernel,
        grid=(b, s // tm),
        in_specs=[x_spec, y_spec, y_spec,
                  _const_spec((1, A_WIDTH)), _const_spec((1, B_WIDTH)),
                  _const_spec(w_out.shape), _const_spec((1, d)),
                  _const_spec(w_up.shape), _const_spec(w_down.shape),
                  _const_spec((1, d))],
        out_specs=x_spec,
        out_shape=jax.ShapeDtypeStruct(x.shape, jnp.float32),
        compiler_params=pltpu.CompilerParams(
            dimension_semantics=("parallel", "parallel"), vmem_limit_bytes=VMEM_LIMIT),
        name="out_proj_mlp",
    )(x, ya, yb, ga, gb, w_out, gm, w_up, w_down, gf)


def _rope_tables(seq_len):
    t = jnp.arange(seq_len, dtype=jnp.int32)
    freqs = ROPE_THETA ** (-jnp.arange(ROPE_HALF, dtype=jnp.float32) / ROPE_HALF)
    zeros = jnp.zeros((seq_len, ROPE_HALF), jnp.float32)
    cos_parts, sa_parts, sb_parts = [], [], []
    for pos in (t // GRID_W, t % GRID_W):
        ang = pos.astype(jnp.float32)[:, None] * freqs[None, :]
        c, sn = jnp.cos(ang), jnp.sin(ang)
        cos_parts += [c, c]
        sa_parts += [-sn, zeros]
        sb_parts += [zeros, sn]
    head = lambda parts: jnp.tile(jnp.concatenate(parts, axis=-1), (1, LANES // HEAD_DIM))
    return head(cos_parts), head(sa_parts), head(sb_parts)


def _extended_w_in(w_in):
    o1 = A_WIDTH
    kv_w = A_KV_HEADS * HEAD_DIM
    cols = [w_in[:, :o1]]
    for base in (o1, o1 + kv_w):
        for hd in range(A_KV_HEADS):
            w = w_in[:, base + hd * HEAD_DIM: base + (hd + 1) * HEAD_DIM]
            cols += [w, w]
    cols.append(w_in[:, o1 + 2 * kv_w:])
    return jnp.concatenate(cols, axis=1).astype(jnp.bfloat16)


def _encoder(x, p):
    qa, ka, va, qb, kb, vb = _in_projection(
        x, p["norm_attn_g"], p["w_in"], p["q_g"], p["k_g"], *p["rope"])
    ya = _global_attention(qa, ka, va)
    yb = _neighbourhood_attention(qb, kb, vb, p["bias"])
    return _out_mlp(x, ya, yb, p["ga"], p["gb"], p["w_out"], p["gm"], p["w_up"],
                    p["w_down"], p["gf"])


def kernel(x_prompt, x_sample, norm_attn_g, w_in, q_norm_g, k_norm_g, nat_rel_bias,
           out_norm_a_g, out_norm_b_g, w_out, norm_mlp_g, w_up, w_down, final_norm_g):
    assert norm_attn_g.shape[0] == 1, "single trunk layer"
    assert x_prompt.shape[1] == x_sample.shape[1]
    row = lambda g: g.reshape(1, -1).astype(jnp.float32)
    pair_gain = lambda g: jnp.tile(row(g), (1, LANES // HEAD_DIM))
    params = {
        "norm_attn_g": row(norm_attn_g[0]),
        "w_in": _extended_w_in(w_in[0]),
        "q_g": pair_gain(q_norm_g[0]),
        "k_g": pair_gain(k_norm_g[0]),
        "rope": _rope_tables(x_prompt.shape[1]),
        "bias": _bias_tables(nat_rel_bias[0]),
        "ga": row(out_norm_a_g[0]),
        "gb": row(out_norm_b_g[0]),
        "w_out": w_out[0].astype(jnp.bfloat16),
        "gm": row(norm_mlp_g[0]),
        "w_up": w_up[0].astype(jnp.bfloat16),
        "w_down": w_down[0].astype(jnp.bfloat16),
        "gf": row(final_norm_g),
    }
    return (_encoder(x_prompt, params), _encoder(x_sample, params))
```

```python
import functools

import jax
import jax.numpy as jnp
from jax import lax
from jax.experimental import pallas as pl
from jax.experimental.pallas import tpu as pltpu

D_MODEL = 1024
HEAD_DIM = 64
LANES = 128
A_WIDTH = 512
B_WIDTH = 512
A_KV_HEADS = 2
A_GROUP = 4
B_HEADS = 8
GRID_W = 64
WIN_ROWS = 8
WIN_COLS = 16
ROPE_THETA = 10000.0
ROPE_HALF = 16
D_FF = 4 * D_MODEL
EPS = 1e-6
SCALE = HEAD_DIM ** -0.5
MASKED = -1e30

N_PAIRS = A_WIDTH // LANES
BAND = WIN_ROWS * GRID_W
PROJ_TM = 1024
ATTN_TQ = 1024
ATTN_TILE = 256
SOFTMAX_ROWS = 16
LOG2E = 1.4426950408889634
MLP_TM = 512
FF_CHUNK = 1024
VMEM_LIMIT = 56 * 1024 * 1024

_NT = (((1,), (1,)), ((), ()))


def _const_spec(shape):
    zeros = (0,) * len(shape)
    return pl.BlockSpec(shape, lambda *_: zeros, pipeline_mode=pl.Buffered(1))


def _rms(x, g):
    return x * lax.rsqrt(jnp.mean(x * x, axis=-1, keepdims=True) + EPS) * g


def _bias_kernel(rb_ref, out_ref):
    h = pl.program_id(0)
    n_dr = 2 * WIN_ROWS - 1
    n_dc = 2 * WIN_COLS - 1
    c = lax.broadcasted_iota(jnp.int32, (GRID_W, LANES), 0)
    lane = lax.broadcasted_iota(jnp.int32, (GRID_W, LANES), 1)
    kc = lane % GRID_W
    first = lane < GRID_W
    d = kc - c + (WIN_COLS - 1)
    cs = jnp.clip(c - WIN_COLS // 2, 0, GRID_W - WIN_COLS)
    valid = (kc >= cs) & (kc < cs + WIN_COLS)
    hits = [d == dd for dd in range(n_dc)]
    base = h * (n_dr * n_dc)
    pair_tiles = []
    for dr in range(n_dr - 1):
        acc = jnp.zeros((GRID_W, LANES), jnp.float32)
        for dd in range(n_dc):
            v0 = rb_ref[base + dr * n_dc + dd]
            v1 = rb_ref[base + (dr + 1) * n_dc + dd]
            acc = jnp.where(hits[dd], jnp.where(first, v0, v1), acc)
        pair_tiles.append(jnp.where(valid, acc * LOG2E, MASKED))
    for a in range(WIN_ROWS):
        for jj in range(WIN_ROWS // 2):
            out_ref[0, a, :, jj * LANES:(jj + 1) * LANES] = pair_tiles[a + 2 * jj]


def _bias_tables(rel_bias):
    flat = rel_bias.reshape(-1).astype(jnp.float32)
    return pl.pallas_call(
        _bias_kernel,
        grid=(B_HEADS,),
        in_specs=[pl.BlockSpec(memory_space=pltpu.SMEM)],
        out_specs=pl.BlockSpec((1, WIN_ROWS, GRID_W, BAND), lambda h: (h // 2, 0, h % 2, 0)),
        out_shape=jax.ShapeDtypeStruct((N_PAIRS, WIN_ROWS, 2 * GRID_W, BAND), jnp.float32),
        name="nat_bias_tables",
    )(flat)


def _proj_kernel(x_ref, g_ref, w_ref, qg_ref, kg_ref, cos_ref, sa_ref, sb_ref,
                 qa_ref, ka_ref, va_ref, qb_ref, kb_ref, vb_ref):
    x = x_ref[0]
    h = _rms(x, g_ref[...]).astype(jnp.bfloat16)
    lo = lax.broadcasted_iota(jnp.int32, (1, LANES), 1) < HEAD_DIM
    cos = cos_ref[...]
    sa = sa_ref[...]
    sb = sb_ref[...]

    def proj(col, width):
        return jnp.dot(h, w_ref[:, col:col + width], preferred_element_type=jnp.float32)

    def norm_rope(t, g):
        sq = t * t
        s_lo = jnp.sum(jnp.where(lo, sq, 0.0), axis=-1, keepdims=True)
        s_hi = jnp.sum(jnp.where(lo, 0.0, sq), axis=-1, keepdims=True)
        ms = jnp.where(lo, s_lo, s_hi) * (1.0 / HEAD_DIM)
        y = t * lax.rsqrt(ms + EPS) * g
        return (y * cos + pltpu.roll(y, LANES - ROPE_HALF, 1) * sa
                + pltpu.roll(y, ROPE_HALF, 1) * sb)

    qa = proj(0, A_WIDTH)
    for p in range(N_PAIRS):
        t = norm_rope(qa[:, p * LANES:(p + 1) * LANES], qg_ref[...])
        qa_ref[0, p] = (t * (SCALE * LOG2E)).astype(jnp.bfloat16)
    kva = proj(A_WIDTH, 2 * LANES)
    ka_ref[0, 0] = norm_rope(kva[:, :LANES], kg_ref[...]).astype(jnp.bfloat16)
    va_ref[0, 0] = kva[:, LANES:].T.astype(jnp.bfloat16)
    col = A_WIDTH + 2 * LANES
    qb = proj(col, B_WIDTH)
    kb = proj(col + B_WIDTH, B_WIDTH)
    vb = proj(col + 2 * B_WIDTH, B_WIDTH)
    for p in range(N_PAIRS):
        sl = slice(p * LANES, (p + 1) * LANES)
        qb_ref[0, p] = (qb[:, sl] * (SCALE * LOG2E)).astype(jnp.bfloat16)
        kb_ref[0, p] = kb[:, sl].astype(jnp.bfloat16)
        vb_ref[0, p] = vb[:, sl].astype(jnp.bfloat16)


def _in_projection(x, norm_g, w_ext, q_g, k_g, cos, sa, sb):
    b, s, d = x.shape
    tm = PROJ_TM
    n_cols = w_ext.shape[1]
    pair = lambda n: jax.ShapeDtypeStruct((b, n, s, LANES), jnp.bfloat16)
    pair_spec = lambda n: pl.BlockSpec((1, n, tm, LANES), lambda bi, si: (bi, 0, si, 0))
    tab_spec = pl.BlockSpec((tm, LANES), lambda bi, si: (si, 0))
    return pl.pallas_call(
        _proj_kernel,
        grid=(b, s // tm),
        in_specs=[
            pl.BlockSpec((1, tm, d), lambda bi, si: (bi, si, 0)),
            _const_spec((1, d)),
            _const_spec((d, n_cols)),
            _const_spec((1, LANES)),
            _const_spec((1, LANES)),
            tab_spec, tab_spec, tab_spec,
        ],
        out_specs=[pair_spec(N_PAIRS), pair_spec(1),
                   pl.BlockSpec((1, 1, LANES, tm), lambda bi, si: (bi, 0, 0, si)),
                   pair_spec(N_PAIRS), pair_spec(N_PAIRS), pair_spec(N_PAIRS)],
        out_shape=[pair(N_PAIRS), pair(1),
                   jax.ShapeDtypeStruct((b, 1, LANES, s), jnp.bfloat16),
                   pair(N_PAIRS), pair(N_PAIRS), pair(N_PAIRS)],
        compiler_params=pltpu.CompilerParams(
            dimension_semantics=("parallel", "parallel"), vmem_limit_bytes=VMEM_LIMIT),
        name="in_projection",
    )(x, norm_g, w_ext, q_g, k_g, cos, sa, sb)


def _exp2_rows(s_ref, m_ref, r0, r1):
    blocks = []
    for a in range(r0, r1, SOFTMAX_ROWS):
        rows = slice(a, a + SOFTMAX_ROWS)
        s = s_ref[rows]
        m = jnp.tile(m_ref[rows], (1, s.shape[1] // LANES))
        blocks.append(jnp.exp2(s - m).astype(jnp.bfloat16))
    return jnp.concatenate(blocks, axis=0)


def _row_max(s_ref, m_ref, r0, r1):
    for a in range(r0, r1, SOFTMAX_ROWS):
        rows = slice(a, a + SOFTMAX_ROWS)
        m = jnp.max(s_ref[rows], axis=-1, keepdims=True)
        m_ref[rows] = jnp.broadcast_to(m, (SOFTMAX_ROWS, LANES))


def _with_ones(v):
    return jnp.concatenate([v, jnp.ones_like(v)], axis=1)


def _normalise(res):
    return res[:, :LANES] / res[:, LANES:]


def _global_attn_kernel(q_ref, k_ref, vt_ref, o_ref, s_a, s_b, m_a, m_b):
    g = pl.program_id(0)
    tq = q_ref.shape[2]
    n_keys = k_ref.shape[2]
    lo = lax.broadcasted_iota(jnp.int32, (1, LANES), 1) < HEAD_DIM
    top = lax.broadcasted_iota(jnp.int32, (LANES, 1), 0) < HEAD_DIM
    sub = 8

    @pl.when(g == 0)
    def _():
        s_b[...] = jnp.zeros_like(s_b)
        m_b[...] = jnp.zeros_like(m_b)

    def step(s_w, m_w, s_r, m_r):
        q = q_ref[0, 0]
        zero = jnp.zeros_like(q)
        qm = jnp.concatenate([jnp.where(lo, q, zero), jnp.where(lo, zero, q)], axis=0)
        vt = vt_ref[0, 0]
        one = jnp.ones_like(vt)
        vt_lo = jnp.where(top, vt, one)
        vt_hi = jnp.where(top, one, vt)
        m_all = jnp.tile(m_r[...], (SOFTMAX_ROWS // sub, 1))
        half = n_keys // 2
        r_lo = r_hi = None
        for i in range(2 * tq // ATTN_TILE):
            cols = slice(i * ATTN_TILE, (i + 1) * ATTN_TILE)
            for rows in (slice(0, half), slice(half, n_keys)):
                s_w[rows, cols] = lax.dot_general(k_ref[0, 0, rows, :], qm[cols], _NT,
                                                  preferred_element_type=jnp.float32)
            for c in range(cols.start, cols.stop, LANES):
                m = jnp.max(s_w[:, c:c + LANES], axis=0, keepdims=True)
                m_w[:, c:c + LANES] = jnp.broadcast_to(m, (sub, LANES))

            keys = slice(i * ATTN_TILE, (i + 1) * ATTN_TILE)

            def probs(c0):
                return jnp.concatenate(
                    [jnp.exp2(s_r[a:a + SOFTMAX_ROWS, c0:c0 + tq]
                              - m_all[:, c0:c0 + tq]).astype(jnp.bfloat16)
                     for a in range(keys.start, keys.stop, SOFTMAX_ROWS)], axis=0)

            d_lo = jnp.dot(vt_lo[:, keys], probs(0), preferred_element_type=jnp.float32)
            d_hi = jnp.dot(vt_hi[:, keys], probs(tq), preferred_element_type=jnp.float32)
            r_lo = d_lo if r_lo is None else r_lo + d_lo
            r_hi = d_hi if r_hi is None else r_hi + d_hi
        ot = jnp.concatenate([r_lo[:HEAD_DIM] / r_lo[HEAD_DIM:],
                              r_hi[HEAD_DIM:] / r_hi[:HEAD_DIM]], axis=0)
        o_ref[0, 0] = ot.T.astype(o_ref.dtype)

    @pl.when(g % 2 == 0)
    def _():
        step(s_a, m_a, s_b, m_b)

    @pl.when(g % 2 == 1)
    def _():
        step(s_b, m_b, s_a, m_a)


def _global_attention(qa, ka, va):
    b, n_pairs, s, _ = qa.shape
    tq = ATTN_TQ
    n_chunks = s // tq
    n_items = b * n_pairs * n_chunks

    def item(g, lag):
        i = jnp.clip(g - lag, 0, n_items - 1)
        return i // (n_pairs * n_chunks), (i // n_chunks) % n_pairs, i % n_chunks

    def q_map(lag):
        def index(g):
            bi, p, c = item(g, lag)
            return bi, p, c, 0
        return index

    def kv_map(lag):
        def index(g):
            bi, _, _ = item(g, lag)
            return bi, 0, 0, 0
        return index

    return pl.pallas_call(
        _global_attn_kernel,
        grid=(n_items + 1,),
        in_specs=[pl.BlockSpec((1, 1, tq, LANES), q_map(0)),
                  pl.BlockSpec((1, 1, s, LANES), kv_map(0)),
                  pl.BlockSpec((1, 1, LANES, s), kv_map(1))],
        out_specs=pl.BlockSpec((1, 1, tq, LANES), q_map(1)),
        out_shape=jax.ShapeDtypeStruct(qa.shape, jnp.bfloat16),
        scratch_shapes=[pltpu.VMEM((s, 2 * tq), jnp.float32),
                        pltpu.VMEM((s, 2 * tq), jnp.float32),
                        pltpu.VMEM((8, 2 * tq), jnp.float32),
                        pltpu.VMEM((8, 2 * tq), jnp.float32)],
        compiler_params=pltpu.CompilerParams(
            dimension_semantics=("arbitrary",), vmem_limit_bytes=VMEM_LIMIT),
        name="global_attention",
    )(qa, ka, va)


def _nat_kernel(q_ref, k_ref, v_ref, bias_ref, o_ref, s_a, s_b, m_a, m_b, *, rows):
    g = pl.program_id(0)
    lo = lax.broadcasted_iota(jnp.int32, (1, LANES), 1) < HEAD_DIM
    stacked = 2 * GRID_W

    @pl.when(g == 0)
    def _():
        s_b[...] = jnp.zeros_like(s_b)
        m_b[...] = jnp.zeros_like(m_b)

    def band_start(r):
        return min(max(r - WIN_ROWS // 2, 0), rows - WIN_ROWS)

    def step(s_w, m_w, s_r, m_r):
        q = q_ref[0, 0]
        zero = jnp.zeros_like(q)
        q_lo = jnp.where(lo, q, zero)
        q_hi = jnp.where(lo, zero, q)
        for r in range(rows):
            rs = band_start(r)
            tok = slice(r * GRID_W, (r + 1) * GRID_W)
            qs = jnp.concatenate([q_lo[tok], q_hi[tok]], axis=0)
            k = k_ref[0, 0, rs * GRID_W:rs * GRID_W + BAND, :]
            s = lax.dot_general(qs, k, _NT, preferred_element_type=jnp.float32)
            s_w[r * stacked:(r + 1) * stacked] = s + bias_ref[0, rs - r + WIN_ROWS - 1]
        _row_max(s_w, m_w, 0, rows * stacked)

        v1 = _with_ones(v_ref[0, 0])
        for r in range(rows):
            rs = band_start(r)
            p = _exp2_rows(s_r, m_r, r * stacked, (r + 1) * stacked)
            o = _normalise(jnp.dot(p, v1[rs * GRID_W:rs * GRID_W + BAND],
                                   preferred_element_type=jnp.float32))
            o_ref[0, 0, r * GRID_W:(r + 1) * GRID_W, :] = jnp.where(
                lo, o[:GRID_W], o[GRID_W:]).astype(o_ref.dtype)

    @pl.when(g % 2 == 0)
    def _():
        step(s_a, m_a, s_b, m_b)

    @pl.when(g % 2 == 1)
    def _():
        step(s_b, m_b, s_a, m_a)


def _neighbourhood_attention(qb, kb, vb, bias):
    b, n_pairs, s, _ = qb.shape
    rows = s // GRID_W
    n_items = n_pairs * b

    def seq_map(lag):
        def index(g):
            i = jnp.clip(g - lag, 0, n_items - 1)
            return i % b, i // b, 0, 0
        return index

    seq = lambda lag: pl.BlockSpec((1, 1, s, LANES), seq_map(lag))
    bias_spec = pl.BlockSpec((1, WIN_ROWS, 2 * GRID_W, BAND),
                             lambda g: (jnp.minimum(g, n_items - 1) // b, 0, 0, 0))
    score_rows = rows * 2 * GRID_W
    return pl.pallas_call(
        functools.partial(_nat_kernel, rows=rows),
        grid=(n_items + 1,),
        in_specs=[seq(0), seq(0), seq(1), bias_spec],
        out_specs=seq(1),
        out_shape=jax.ShapeDtypeStruct(qb.shape, jnp.bfloat16),
        scratch_shapes=[pltpu.VMEM((score_rows, BAND), jnp.float32),
                        pltpu.VMEM((score_rows, BAND), jnp.float32),
                        pltpu.VMEM((score_rows, LANES), jnp.float32),
                        pltpu.VMEM((score_rows, LANES), jnp.float32)],
        compiler_params=pltpu.CompilerParams(
            dimension_semantics=("arbitrary",), vmem_limit_bytes=VMEM_LIMIT),
        name="neighbourhood_attention",
    )(qb, kb, vb, bias)


def _mlp_kernel(x_ref, ya_ref, yb_ref, ga_ref, gb_ref, wo_ref, gm_ref, wu_ref, wd_ref,
                gf_ref, o_ref):
    ya = jnp.concatenate([ya_ref[0, p] for p in range(N_PAIRS)], axis=-1)
    yb = jnp.concatenate([yb_ref[0, p] for p in range(N_PAIRS)], axis=-1)
    y = jnp.concatenate([_rms(ya.astype(jnp.float32), ga_ref[...]),
                         _rms(yb.astype(jnp.float32), gb_ref[...])], axis=-1)
    x1 = x_ref[0] + jnp.dot(y.astype(jnp.bfloat16), wo_ref[...],
                            preferred_element_type=jnp.float32)
    h = _rms(x1, gm_ref[...]).astype(jnp.bfloat16)
    acc = x1
    for c in range(D_FF // FF_CHUNK):
        sl = slice(c * FF_CHUNK, (c + 1) * FF_CHUNK)
        u = jnp.dot(h, wu_ref[:, sl], preferred_element_type=jnp.float32)
        u = jnp.square(jnp.maximum(u, 0.0)).astype(jnp.bfloat16)
        acc = acc + jnp.dot(u, wd_ref[sl, :], preferred_element_type=jnp.float32)
    o_ref[0] = _rms(acc, gf_ref[...])


def _out_mlp(x, ya, yb, ga, gb, w_out, gm, w_up, w_down, gf):
    b, s, d = x.shape
    tm = MLP_TM
    x_spec = pl.BlockSpec((1, tm, d), lambda bi, si: (bi, si, 0))
    y_spec = pl.BlockSpec((1, N_PAIRS, tm, LANES), lambda bi, si: (bi, 0, si, 0))
    return pl.pallas_call(
        _mlp_kernel,
        grid=(b, s // tm),
        in_specs=[x_spec, y_spec, y_spec,
                  _const_spec((1, A_WIDTH)), _const_spec((1, B_WIDTH)),
                  _const_spec(w_out.shape), _const_spec((1, d)),
                  _const_spec(w_up.shape), _const_spec(w_down.shape),
                  _const_spec((1, d))],
        out_specs=x_spec,
        out_shape=jax.ShapeDtypeStruct(x.shape, jnp.float32),
        compiler_params=pltpu.CompilerParams(
            dimension_semantics=("parallel", "parallel"), vmem_limit_bytes=VMEM_LIMIT),
        name="out_proj_mlp",
    )(x, ya, yb, ga, gb, w_out, gm, w_up, w_down, gf)


def _rope_tables(seq_len):
    t = jnp.arange(seq_len, dtype=jnp.int32)
    freqs = ROPE_THETA ** (-jnp.arange(ROPE_HALF, dtype=jnp.float32) / ROPE_HALF)
    zeros = jnp.zeros((seq_len, ROPE_HALF), jnp.float32)
    cos_parts, sa_parts, sb_parts = [], [], []
    for pos in (t // GRID_W, t % GRID_W):
        ang = pos.astype(jnp.float32)[:, None] * freqs[None, :]
        c, sn = jnp.cos(ang), jnp.sin(ang)
        cos_parts += [c, c]
        sa_parts += [-sn, zeros]
        sb_parts += [zeros, sn]
    head = lambda parts: jnp.tile(jnp.concatenate(parts, axis=-1), (1, LANES // HEAD_DIM))
    return head(cos_parts), head(sa_parts), head(sb_parts)


def _pair_heads(a, axis):
    shape = a.shape
    a = a.reshape(shape[:axis] + (A_KV_HEADS, A_GROUP, HEAD_DIM) + shape[axis + 1:])
    return jnp.swapaxes(a, axis, axis + 1).reshape(shape)


def _paired_w_in(w_in):
    qa = _pair_heads(w_in[:, :A_WIDTH], 1)
    return jnp.concatenate([qa, w_in[:, A_WIDTH:]], axis=1).astype(jnp.bfloat16)


def _encoder(x, p):
    qa, ka, va, qb, kb, vb = _in_projection(
        x, p["norm_attn_g"], p["w_in"], p["q_g"], p["k_g"], *p["rope"])
    ya = _global_attention(qa, ka, va)
    yb = _neighbourhood_attention(qb, kb, vb, p["bias"])
    return _out_mlp(x, ya, yb, p["ga"], p["gb"], p["w_out"], p["gm"], p["w_up"],
                    p["w_down"], p["gf"])


def kernel(x_prompt, x_sample, norm_attn_g, w_in, q_norm_g, k_norm_g, nat_rel_bias,
           out_norm_a_g, out_norm_b_g, w_out, norm_mlp_g, w_up, w_down, final_norm_g):
    assert norm_attn_g.shape[0] == 1, "single trunk layer"
    assert x_prompt.shape[1] == x_sample.shape[1]
    row = lambda g: g.reshape(1, -1).astype(jnp.float32)
    pair_gain = lambda g: jnp.tile(row(g), (1, LANES // HEAD_DIM))
    params = {
        "norm_attn_g": row(norm_attn_g[0]),
        "w_in": _paired_w_in(w_in[0]),
        "q_g": pair_gain(q_norm_g[0]),
        "k_g": pair_gain(k_norm_g[0]),
        "rope": _rope_tables(x_prompt.shape[1]),
        "bias": _bias_tables(nat_rel_bias[0]),
        "ga": row(_pair_heads(out_norm_a_g[0], 0)),
        "gb": row(out_norm_b_g[0]),
        "w_out": jnp.concatenate([_pair_heads(w_out[0, :A_WIDTH], 0), w_out[0, A_WIDTH:]],
                                 axis=0).astype(jnp.bfloat16),
        "gm": row(norm_mlp_g[0]),
        "w_up": w_up[0].astype(jnp.bfloat16),
        "w_down": w_down[0].astype(jnp.bfloat16),
        "gf": row(final_norm_g),
    }
    return (_encoder(x_prompt, params), _encoder(x_sample, params))
```

```python
import functools

import jax
import jax.numpy as jnp
from jax import lax
from jax.experimental import pallas as pl
from jax.experimental.pallas import tpu as pltpu

D_MODEL = 1024
HEAD_DIM = 64
LANES = 128
A_WIDTH = 512
B_WIDTH = 512
A_KV_HEADS = 2
A_GROUP = 4
B_HEADS = 8
GRID_W = 64
WIN_ROWS = 8
WIN_COLS = 16
ROPE_THETA = 10000.0
ROPE_HALF = 16
D_FF = 4 * D_MODEL
EPS = 1e-6
SCALE = HEAD_DIM ** -0.5
MASKED = -1e30

N_PAIRS = A_WIDTH // LANES
BAND = WIN_ROWS * GRID_W
PROJ_TM = 1024
ATTN_TQ = 1024
ATTN_TILE = 256
SOFTMAX_ROWS = 16
LOG2E = 1.4426950408889634
MLP_TM = 1024
FF_CHUNK = 1024
VMEM_LIMIT = 56 * 1024 * 1024

_NT = (((1,), (1,)), ((), ()))


def _const_spec(shape):
    zeros = (0,) * len(shape)
    return pl.BlockSpec(shape, lambda *_: zeros, pipeline_mode=pl.Buffered(1))


def _rms(x, g):
    return x * lax.rsqrt(jnp.mean(x * x, axis=-1, keepdims=True) + EPS) * g


def _bias_kernel(rb_ref, out_ref):
    h = pl.program_id(0)
    n_dr = 2 * WIN_ROWS - 1
    n_dc = 2 * WIN_COLS - 1
    c = lax.broadcasted_iota(jnp.int32, (GRID_W, LANES), 0)
    lane = lax.broadcasted_iota(jnp.int32, (GRID_W, LANES), 1)
    kc = lane % GRID_W
    first = lane < GRID_W
    d = kc - c + (WIN_COLS - 1)
    cs = jnp.clip(c - WIN_COLS // 2, 0, GRID_W - WIN_COLS)
    valid = (kc >= cs) & (kc < cs + WIN_COLS)
    hits = [d == dd for dd in range(n_dc)]
    base = h * (n_dr * n_dc)
    pair_tiles = []
    for dr in range(n_dr - 1):
        acc = jnp.zeros((GRID_W, LANES), jnp.float32)
        for dd in range(n_dc):
            v0 = rb_ref[base + dr * n_dc + dd]
            v1 = rb_ref[base + (dr + 1) * n_dc + dd]
            acc = jnp.where(hits[dd], jnp.where(first, v0, v1), acc)
        pair_tiles.append(jnp.where(valid, acc * LOG2E, MASKED))
    for a in range(WIN_ROWS):
        for jj in range(WIN_ROWS // 2):
            out_ref[0, a, :, jj * LANES:(jj + 1) * LANES] = pair_tiles[a + 2 * jj]


def _bias_tables(rel_bias):
    flat = rel_bias.reshape(-1).astype(jnp.float32)
    return pl.pallas_call(
        _bias_kernel,
        grid=(B_HEADS,),
        in_specs=[pl.BlockSpec(memory_space=pltpu.SMEM)],
        out_specs=pl.BlockSpec((1, WIN_ROWS, GRID_W, BAND), lambda h: (h // 2, 0, h % 2, 0)),
        out_shape=jax.ShapeDtypeStruct((N_PAIRS, WIN_ROWS, 2 * GRID_W, BAND), jnp.float32),
        name="nat_bias_tables",
    )(flat)


def _proj_kernel(x_ref, g_ref, w_ref, qg_ref, kg_ref, cos_ref, sa_ref, sb_ref,
                 qa_ref, ka_ref, va_ref, qb_ref, kb_ref, vb_ref):
    x = x_ref[0]
    h = _rms(x, g_ref[...]).astype(jnp.bfloat16)
    lo = lax.broadcasted_iota(jnp.int32, (1, LANES), 1) < HEAD_DIM
    cos = cos_ref[...]
    sa = sa_ref[...]
    sb = sb_ref[...]

    def proj(col, width):
        return jnp.dot(h, w_ref[:, col:col + width], preferred_element_type=jnp.float32)

    def norm_rope(t, g):
        sq = t * t
        s_lo = jnp.sum(jnp.where(lo, sq, 0.0), axis=-1, keepdims=True)
        s_hi = jnp.sum(jnp.where(lo, 0.0, sq), axis=-1, keepdims=True)
        ms = jnp.where(lo, s_lo, s_hi) * (1.0 / HEAD_DIM)
        y = t * lax.rsqrt(ms + EPS) * g
        return (y * cos + pltpu.roll(y, LANES - ROPE_HALF, 1) * sa
                + pltpu.roll(y, ROPE_HALF, 1) * sb)

    qa = proj(0, A_WIDTH)
    for p in range(N_PAIRS):
        t = norm_rope(qa[:, p * LANES:(p + 1) * LANES], qg_ref[...])
        qa_ref[0, p] = (t * (SCALE * LOG2E)).astype(jnp.bfloat16)
    kva = proj(A_WIDTH, 2 * LANES)
    ka_ref[0, 0] = norm_rope(kva[:, :LANES], kg_ref[...]).astype(jnp.bfloat16)
    va_ref[0, 0] = kva[:, LANES:].T.astype(jnp.bfloat16)
    col = A_WIDTH + 2 * LANES
    qb = proj(col, B_WIDTH)
    kb = proj(col + B_WIDTH, B_WIDTH)
    vb = proj(col + 2 * B_WIDTH, B_WIDTH)
    for p in range(N_PAIRS):
        sl = slice(p * LANES, (p + 1) * LANES)
        qb_ref[0, p] = (qb[:, sl] * (SCALE * LOG2E)).astype(jnp.bfloat16)
        kb_ref[0, p] = kb[:, sl].astype(jnp.bfloat16)
        vb_ref[0, p] = vb[:, sl].astype(jnp.bfloat16)


def _in_projection(x, norm_g, w_ext, q_g, k_g, cos, sa, sb):
    b, s, d = x.shape
    tm = PROJ_TM
    n_cols = w_ext.shape[1]
    pair = lambda n: jax.ShapeDtypeStruct((b, n, s, LANES), jnp.bfloat16)
    pair_spec = lambda n: pl.BlockSpec((1, n, tm, LANES), lambda bi, si: (bi, 0, si, 0))
    tab_spec = pl.BlockSpec((tm, LANES), lambda bi, si: (si, 0))
    return pl.pallas_call(
        _proj_kernel,
        grid=(b, s // tm),
        in_specs=[
            pl.BlockSpec((1, tm, d), lambda bi, si: (bi, si, 0)),
            _const_spec((1, d)),
            _const_spec((d, n_cols)),
            _const_spec((1, LANES)),
            _const_spec((1, LANES)),
            tab_spec, tab_spec, tab_spec,
        ],
        out_specs=[pair_spec(N_PAIRS), pair_spec(1),
                   pl.BlockSpec((1, 1, LANES, tm), lambda bi, si: (bi, 0, 0, si)),
                   pair_spec(N_PAIRS), pair_spec(N_PAIRS), pair_spec(N_PAIRS)],
        out_shape=[pair(N_PAIRS), pair(1),
                   jax.ShapeDtypeStruct((b, 1, LANES, s), jnp.bfloat16),
                   pair(N_PAIRS), pair(N_PAIRS), pair(N_PAIRS)],
        compiler_params=pltpu.CompilerParams(
            dimension_semantics=("parallel", "parallel"), vmem_limit_bytes=VMEM_LIMIT),
        name="in_projection",
    )(x, norm_g, w_ext, q_g, k_g, cos, sa, sb)


def _exp2_rows(s_ref, m_ref, r0, r1):
    blocks = []
    for a in range(r0, r1, SOFTMAX_ROWS):
        rows = slice(a, a + SOFTMAX_ROWS)
        s = s_ref[rows]
        m = jnp.tile(m_ref[rows], (1, s.shape[1] // LANES))
        blocks.append(jnp.exp2(s - m).astype(jnp.bfloat16))
    return jnp.concatenate(blocks, axis=0)


def _row_max(s_ref, m_ref, r0, r1):
    for a in range(r0, r1, SOFTMAX_ROWS):
        rows = slice(a, a + SOFTMAX_ROWS)
        m = jnp.max(s_ref[rows], axis=-1, keepdims=True)
        m_ref[rows] = jnp.broadcast_to(m, (SOFTMAX_ROWS, LANES))


def _with_ones(v):
    return jnp.concatenate([v, jnp.ones_like(v)], axis=1)


def _normalise(res):
    return res[:, :LANES] / res[:, LANES:]


def _global_attn_kernel(q_ref, k_ref, vt_ref, o_ref, s_a, s_b, m_a, m_b):
    g = pl.program_id(0)
    tq = q_ref.shape[2]
    n_keys = k_ref.shape[2]
    lo = lax.broadcasted_iota(jnp.int32, (1, LANES), 1) < HEAD_DIM
    top = lax.broadcasted_iota(jnp.int32, (LANES, 1), 0) < HEAD_DIM
    sub = 8

    @pl.when(g == 0)
    def _():
        s_b[...] = jnp.zeros_like(s_b)
        m_b[...] = jnp.zeros_like(m_b)

    def step(s_w, m_w, s_r, m_r):
        q = q_ref[0, 0]
        zero = jnp.zeros_like(q)
        qm = jnp.concatenate([jnp.where(lo, q, zero), jnp.where(lo, zero, q)], axis=0)
        vt = vt_ref[0, 0]
        one = jnp.ones_like(vt)
        vt_lo = jnp.where(top, vt, one)
        vt_hi = jnp.where(top, one, vt)
        m_all = jnp.tile(m_r[...], (SOFTMAX_ROWS // sub, 1))
        half = n_keys // 2
        r_lo = r_hi = None
        for i in range(2 * tq // ATTN_TILE):
            cols = slice(i * ATTN_TILE, (i + 1) * ATTN_TILE)
            for rows in (slice(0, half), slice(half, n_keys)):
                s_w[rows, cols] = lax.dot_general(k_ref[0, 0, rows, :], qm[cols], _NT,
                                                  preferred_element_type=jnp.float32)
            for c in range(cols.start, cols.stop, LANES):
                m = jnp.max(s_w[:, c:c + LANES], axis=0, keepdims=True)
                m_w[:, c:c + LANES] = jnp.broadcast_to(m, (sub, LANES))

            keys = slice(i * ATTN_TILE, (i + 1) * ATTN_TILE)

            def probs(c0):
                return jnp.concatenate(
                    [jnp.exp2(s_r[a:a + SOFTMAX_ROWS, c0:c0 + tq]
                              - m_all[:, c0:c0 + tq]).astype(jnp.bfloat16)
                     for a in range(keys.start, keys.stop, SOFTMAX_ROWS)], axis=0)

            d_lo = jnp.dot(vt_lo[:, keys], probs(0), preferred_element_type=jnp.float32)
            d_hi = jnp.dot(vt_hi[:, keys], probs(tq), preferred_element_type=jnp.float32)
            r_lo = d_lo if r_lo is None else r_lo + d_lo
            r_hi = d_hi if r_hi is None else r_hi + d_hi
        ot = jnp.concatenate([r_lo[:HEAD_DIM] / r_lo[HEAD_DIM:],
                              r_hi[HEAD_DIM:] / r_hi[:HEAD_DIM]], axis=0)
        o_ref[0, 0] = ot.T.astype(o_ref.dtype)

    @pl.when(g % 2 == 0)
    def _():
        step(s_a, m_a, s_b, m_b)

    @pl.when(g % 2 == 1)
    def _():
        step(s_b, m_b, s_a, m_a)


def _global_attention(qa, ka, va):
    b, n_pairs, s, _ = qa.shape
    tq = ATTN_TQ
    n_chunks = s // tq
    n_items = b * n_pairs * n_chunks

    def item(g, lag):
        i = jnp.clip(g - lag, 0, n_items - 1)
        return i // (n_pairs * n_chunks), (i // n_chunks) % n_pairs, i % n_chunks

    def q_map(lag):
        def index(g):
            bi, p, c = item(g, lag)
            return bi, p, c, 0
        return index

    def kv_map(lag):
        def index(g):
            bi, _, _ = item(g, lag)
            return bi, 0, 0, 0
        return index

    return pl.pallas_call(
        _global_attn_kernel,
        grid=(n_items + 1,),
        in_specs=[pl.BlockSpec((1, 1, tq, LANES), q_map(0)),
                  pl.BlockSpec((1, 1, s, LANES), kv_map(0)),
                  pl.BlockSpec((1, 1, LANES, s), kv_map(1))],
        out_specs=pl.BlockSpec((1, 1, tq, LANES), q_map(1)),
        out_shape=jax.ShapeDtypeStruct(qa.shape, jnp.bfloat16),
        scratch_shapes=[pltpu.VMEM((s, 2 * tq + LANES), jnp.float32),
                        pltpu.VMEM((s, 2 * tq + LANES), jnp.float32),
                        pltpu.VMEM((8, 2 * tq), jnp.float32),
                        pltpu.VMEM((8, 2 * tq), jnp.float32)],
        compiler_params=pltpu.CompilerParams(
            dimension_semantics=("arbitrary",), vmem_limit_bytes=VMEM_LIMIT),
        name="global_attention",
    )(qa, ka, va)


def _nat_kernel(q_ref, k_ref, v_ref, bias_ref, o_ref, s_a, s_b, m_a, m_b, *, rows):
    g = pl.program_id(0)
    lo = lax.broadcasted_iota(jnp.int32, (1, LANES), 1) < HEAD_DIM
    stacked = 2 * GRID_W

    @pl.when(g == 0)
    def _():
        s_b[...] = jnp.zeros_like(s_b)
        m_b[...] = jnp.zeros_like(m_b)

    def band_start(r):
        return min(max(r - WIN_ROWS // 2, 0), rows - WIN_ROWS)

    def step(s_w, m_w, s_r, m_r):
        q = q_ref[0, 0]
        zero = jnp.zeros_like(q)
        q_lo = jnp.where(lo, q, zero)
        q_hi = jnp.where(lo, zero, q)
        for r in range(rows):
            rs = band_start(r)
            tok = slice(r * GRID_W, (r + 1) * GRID_W)
            qs = jnp.concatenate([q_lo[tok], q_hi[tok]], axis=0)
            k = k_ref[0, 0, rs * GRID_W:rs * GRID_W + BAND, :]
            s = lax.dot_general(qs, k, _NT, preferred_element_type=jnp.float32)
            s_w[r * stacked:(r + 1) * stacked] = s + bias_ref[0, rs - r + WIN_ROWS - 1]
        _row_max(s_w, m_w, 0, rows * stacked)

        v1 = _with_ones(v_ref[0, 0])
        for r in range(rows):
            rs = band_start(r)
            p = _exp2_rows(s_r, m_r, r * stacked, (r + 1) * stacked)
            o = _normalise(jnp.dot(p, v1[rs * GRID_W:rs * GRID_W + BAND],
                                   preferred_element_type=jnp.float32))
            o_ref[0, 0, r * GRID_W:(r + 1) * GRID_W, :] = jnp.where(
                lo, o[:GRID_W], o[GRID_W:]).astype(o_ref.dtype)

    @pl.when(g % 2 == 0)
    def _():
        step(s_a, m_a, s_b, m_b)

    @pl.when(g % 2 == 1)
    def _():
        step(s_b, m_b, s_a, m_a)


def _neighbourhood_attention(qb, kb, vb, bias):
    b, n_pairs, s, _ = qb.shape
    rows = s // GRID_W
    n_items = n_pairs * b

    def seq_map(lag):
        def index(g):
            i = jnp.clip(g - lag, 0, n_items - 1)
            return i % b, i // b, 0, 0
        return index

    seq = lambda lag: pl.BlockSpec((1, 1, s, LANES), seq_map(lag))
    bias_spec = pl.BlockSpec((1, WIN_ROWS, 2 * GRID_W, BAND),
                             lambda g: (jnp.minimum(g, n_items - 1) // b, 0, 0, 0))
    score_rows = rows * 2 * GRID_W
    return pl.pallas_call(
        functools.partial(_nat_kernel, rows=rows),
        grid=(n_items + 1,),
        in_specs=[seq(0), seq(0), seq(1), bias_spec],
        out_specs=seq(1),
        out_shape=jax.ShapeDtypeStruct(qb.shape, jnp.bfloat16),
        scratch_shapes=[pltpu.VMEM((score_rows, BAND), jnp.float32),
                        pltpu.VMEM((score_rows, BAND), jnp.float32),
                        pltpu.VMEM((score_rows, LANES), jnp.float32),
                        pltpu.VMEM((score_rows, LANES), jnp.float32)],
        compiler_params=pltpu.CompilerParams(
            dimension_semantics=("arbitrary",), vmem_limit_bytes=VMEM_LIMIT),
        name="neighbourhood_attention",
    )(qb, kb, vb, bias)


def _mlp_kernel(x_ref, ya_ref, yb_ref, ga_ref, gb_ref, wo_ref, gm_ref, wu_ref, wd_ref,
                gf_ref, o_ref):
    ya = jnp.concatenate([ya_ref[0, p] for p in range(N_PAIRS)], axis=-1)
    yb = jnp.concatenate([yb_ref[0, p] for p in range(N_PAIRS)], axis=-1)
    y = jnp.concatenate([_rms(ya.astype(jnp.float32), ga_ref[...]),
                         _rms(yb.astype(jnp.float32), gb_ref[...])], axis=-1)
    x1 = x_ref[0] + jnp.dot(y.astype(jnp.bfloat16), wo_ref[...],
                            preferred_element_type=jnp.float32)
    h = _rms(x1, gm_ref[...]).astype(jnp.bfloat16)
    acc = x1
    for c in range(D_FF // FF_CHUNK):
        sl = slice(c * FF_CHUNK, (c + 1) * FF_CHUNK)
        u = jnp.dot(h, wu_ref[:, sl], preferred_element_type=jnp.float32)
        u = jnp.square(jnp.maximum(u, 0.0)).astype(jnp.bfloat16)
        acc = acc + jnp.dot(u, wd_ref[sl, :], preferred_element_type=jnp.float32)
    o_ref[0] = _rms(acc, gf_ref[...])


def _out_mlp(x, ya, yb, ga, gb, w_out, gm, w_up, w_down, gf):
    b, s, d = x.shape
    tm = MLP_TM
    x_spec = pl.BlockSpec((1, tm, d), lambda bi, si: (bi, si, 0))
    y_spec = pl.BlockSpec((1, N_PAIRS, tm, LANES), lambda bi, si: (bi, 0, si, 0))
    return pl.pallas_call(
        _mlp_kernel,
        grid=(b, s // tm),
        in_specs=[x_spec, y_spec, y_spec,
                  _const_spec((1, A_WIDTH)), _const_spec((1, B_WIDTH)),
                  _const_spec(w_out.shape), _const_spec((1, d)),
                  _const_spec(w_up.shape), _const_spec(w_down.shape),
                  _const_spec((1, d))],
        out_specs=x_spec,
        out_shape=jax.ShapeDtypeStruct(x.shape, jnp.float32),
        compiler_params=pltpu.CompilerParams(
            dimension_semantics=("parallel", "parallel"), vmem_limit_bytes=VMEM_LIMIT),
        name="out_proj_mlp",
    )(x, ya, yb, ga, gb, w_out, gm, w_up, w_down, gf)


def _rope_tables(seq_len):
    t = jnp.arange(seq_len, dtype=jnp.int32)
    freqs = ROPE_THETA ** (-jnp.arange(ROPE_HALF, dtype=jnp.float32) / ROPE_HALF)
    zeros = jnp.zeros((seq_len, ROPE_HALF), jnp.float32)
    cos_parts, sa_parts, sb_parts = [], [], []
    for pos in (t // GRID_W, t % GRID_W):
        ang = pos.astype(jnp.float32)[:, None] * freqs[None, :]
        c, sn = jnp.cos(ang), jnp.sin(ang)
        cos_parts += [c, c]
        sa_parts += [-sn, zeros]
        sb_parts += [zeros, sn]
    head = lambda parts: jnp.tile(jnp.concatenate(parts, axis=-1), (1, LANES // HEAD_DIM))
    return head(cos_parts), head(sa_parts), head(sb_parts)


def _pair_heads(a, axis):
    shape = a.shape
    a = a.reshape(shape[:axis] + (A_KV_HEADS, A_GROUP, HEAD_DIM) + shape[axis + 1:])
    return jnp.swapaxes(a, axis, axis + 1).reshape(shape)


def _paired_w_in(w_in):
    qa = _pair_heads(w_in[:, :A_WIDTH], 1)
    return jnp.concatenate([qa, w_in[:, A_WIDTH:]], axis=1).astype(jnp.bfloat16)


def _encoder(x, p):
    qa, ka, va, qb, kb, vb = _in_projection(
        x, p["norm_attn_g"], p["w_in"], p["q_g"], p["k_g"], *p["rope"])
    ya = _global_attention(qa, ka, va)
    yb = _neighbourhood_attention(qb, kb, vb, p["bias"])
    return _out_mlp(x, ya, yb, p["ga"], p["gb"], p["w_out"], p["gm"], p["w_up"],
                    p["w_down"], p["gf"])


def kernel(x_prompt, x_sample, norm_attn_g, w_in, q_norm_g, k_norm_g, nat_rel_bias,
           out_norm_a_g, out_norm_b_g, w_out, norm_mlp_g, w_up, w_down, final_norm_g):
    assert norm_attn_g.shape[0] == 1, "single trunk layer"
    assert x_prompt.shape[1] == x_sample.shape[1]
    row = lambda g: g.reshape(1, -1).astype(jnp.float32)
    pair_gain = lambda g: jnp.tile(row(g), (1, LANES // HEAD_DIM))
    params = {
        "norm_attn_g": row(norm_attn_g[0]),
        "w_in": _paired_w_in(w_in[0]),
        "q_g": pair_gain(q_norm_g[0]),
        "k_g": pair_gain(k_norm_g[0]),
        "rope": _rope_tables(x_prompt.shape[1]),
        "bias": _bias_tables(nat_rel_bias[0]),
        "ga": row(_pair_heads(out_norm_a_g[0], 0)),
        "gb": row(out_norm_b_g[0]),
        "w_out": jnp.concatenate([_pair_heads(w_out[0, :A_WIDTH], 0), w_out[0, A_WIDTH:]],
                                 axis=0).astype(jnp.bfloat16),
        "gm": row(norm_mlp_g[0]),
        "w_up": w_up[0].astype(jnp.bfloat16),
        "w_down": w_down[0].astype(jnp.bfloat16),
        "gf": row(final_norm_g),
    }
    return (_encoder(x_prompt, params), _encoder(x_sample, params))
```

```python
import functools

import jax
import jax.numpy as jnp
from jax import lax
from jax.experimental import pallas as pl
from jax.experimental.pallas import tpu as pltpu

D_MODEL = 1024
HEAD_DIM = 64
LANES = 128
A_WIDTH = 512
B_WIDTH = 512
A_KV_HEADS = 2
A_GROUP = 4
B_HEADS = 8
GRID_W = 64
WIN_ROWS = 8
WIN_COLS = 16
ROPE_THETA = 10000.0
ROPE_HALF = 16
D_FF = 4 * D_MODEL
EPS = 1e-6
SCALE = HEAD_DIM ** -0.5
MASKED = -1e30

N_PAIRS = A_WIDTH // LANES
BAND = WIN_ROWS * GRID_W
PROJ_TM = 1024
ATTN_TQ = 1024
ATTN_TILE = 256
PV_TILES = 1
SOFTMAX_ROWS = 16
LOG2E = 1.4426950408889634
MLP_TM = 1024
FF_CHUNK = 1024
VMEM_LIMIT = 56 * 1024 * 1024

_NT = (((1,), (1,)), ((), ()))


def _const_spec(shape):
    zeros = (0,) * len(shape)
    return pl.BlockSpec(shape, lambda *_: zeros, pipeline_mode=pl.Buffered(1))


def _rms(x, g):
    return x * lax.rsqrt(jnp.mean(x * x, axis=-1, keepdims=True) + EPS) * g


def _bias_kernel(rb_ref, out_ref):
    h = pl.program_id(0)
    n_dr = 2 * WIN_ROWS - 1
    n_dc = 2 * WIN_COLS - 1
    c = lax.broadcasted_iota(jnp.int32, (GRID_W, LANES), 0)
    lane = lax.broadcasted_iota(jnp.int32, (GRID_W, LANES), 1)
    kc = lane % GRID_W
    first = lane < GRID_W
    d = kc - c + (WIN_COLS - 1)
    cs = jnp.clip(c - WIN_COLS // 2, 0, GRID_W - WIN_COLS)
    valid = (kc >= cs) & (kc < cs + WIN_COLS)
    hits = [d == dd for dd in range(n_dc)]
    base = h * (n_dr * n_dc)
    pair_tiles = []
    for dr in range(n_dr - 1):
        acc = jnp.zeros((GRID_W, LANES), jnp.float32)
        for dd in range(n_dc):
            v0 = rb_ref[base + dr * n_dc + dd]
            v1 = rb_ref[base + (dr + 1) * n_dc + dd]
            acc = jnp.where(hits[dd], jnp.where(first, v0, v1), acc)
        pair_tiles.append(jnp.where(valid, acc * LOG2E, MASKED))
    for a in range(WIN_ROWS):
        for jj in range(WIN_ROWS // 2):
            out_ref[0, a, :, jj * LANES:(jj + 1) * LANES] = pair_tiles[a + 2 * jj]


def _bias_tables(rel_bias):
    flat = rel_bias.reshape(-1).astype(jnp.float32)
    return pl.pallas_call(
        _bias_kernel,
        grid=(B_HEADS,),
        in_specs=[pl.BlockSpec(memory_space=pltpu.SMEM)],
        out_specs=pl.BlockSpec((1, WIN_ROWS, GRID_W, BAND), lambda h: (h // 2, 0, h % 2, 0)),
        out_shape=jax.ShapeDtypeStruct((N_PAIRS, WIN_ROWS, 2 * GRID_W, BAND), jnp.float32),
        name="nat_bias_tables",
    )(flat)


def _proj_kernel(x_ref, g_ref, w_ref, qg_ref, kg_ref, cos_ref, sa_ref, sb_ref,
                 qa_ref, ka_ref, va_ref, qb_ref, kb_ref, vb_ref, raw_a, raw_b):
    g = pl.program_id(0)
    lo = lax.broadcasted_iota(jnp.int32, (1, LANES), 1) < HEAD_DIM
    qk_width = A_WIDTH + LANES

    @pl.when(g == 0)
    def _():
        raw_b[...] = jnp.zeros_like(raw_b)

    def norm_rope(t, gain):
        sq = t * t
        s_lo = jnp.sum(jnp.where(lo, sq, 0.0), axis=-1, keepdims=True)
        s_hi = jnp.sum(jnp.where(lo, 0.0, sq), axis=-1, keepdims=True)
        ms = jnp.where(lo, s_lo, s_hi) * (1.0 / HEAD_DIM)
        y = t * lax.rsqrt(ms + EPS) * gain
        return (y * cos_ref[...] + pltpu.roll(y, LANES - ROPE_HALF, 1) * sa_ref[...]
                + pltpu.roll(y, ROPE_HALF, 1) * sb_ref[...])

    def step(raw_w, raw_r):
        h = _rms(x_ref[0], g_ref[...]).astype(jnp.bfloat16)

        def proj(col, width):
            return jnp.dot(h, w_ref[:, col:col + width], preferred_element_type=jnp.float32)

        raw_w[:, :A_WIDTH] = proj(0, A_WIDTH)
        kva = proj(A_WIDTH, 2 * LANES)
        raw_w[:, A_WIDTH:qk_width] = kva[:, :LANES]
        va_ref[0, 0] = kva[:, LANES:].T.astype(jnp.bfloat16)
        col = A_WIDTH + 2 * LANES
        qb = proj(col, B_WIDTH)
        kb = proj(col + B_WIDTH, B_WIDTH)
        vb = proj(col + 2 * B_WIDTH, B_WIDTH)
        for p in range(N_PAIRS):
            sl = slice(p * LANES, (p + 1) * LANES)
            qb_ref[0, p] = (qb[:, sl] * (SCALE * LOG2E)).astype(jnp.bfloat16)
            kb_ref[0, p] = kb[:, sl].astype(jnp.bfloat16)
            vb_ref[0, p] = vb[:, sl].astype(jnp.bfloat16)

        for p in range(N_PAIRS):
            t = norm_rope(raw_r[:, p * LANES:(p + 1) * LANES], qg_ref[...])
            qa_ref[0, p] = (t * (SCALE * LOG2E)).astype(jnp.bfloat16)
        ka_ref[0, 0] = norm_rope(raw_r[:, A_WIDTH:qk_width], kg_ref[...]).astype(jnp.bfloat16)

    @pl.when(g % 2 == 0)
    def _():
        step(raw_a, raw_b)

    @pl.when(g % 2 == 1)
    def _():
        step(raw_b, raw_a)


def _in_projection(x, norm_g, w_ext, q_g, k_g, cos, sa, sb):
    b, s, d = x.shape
    tm = PROJ_TM
    n_tiles = s // tm
    n_items = b * n_tiles
    n_cols = w_ext.shape[1]

    def item(g, lag):
        i = jnp.clip(g - lag, 0, n_items - 1)
        return i // n_tiles, i % n_tiles

    def pair_spec(n, lag):
        def index(g):
            bi, si = item(g, lag)
            return bi, 0, si, 0
        return pl.BlockSpec((1, n, tm, LANES), index)

    def x_index(g):
        bi, si = item(g, 0)
        return bi, si, 0

    def vt_index(g):
        bi, si = item(g, 0)
        return bi, 0, 0, si

    pair = lambda n: jax.ShapeDtypeStruct((b, n, s, LANES), jnp.bfloat16)
    tab_spec = pl.BlockSpec((tm, LANES), lambda g: (item(g, 1)[1], 0))
    raw = pltpu.VMEM((tm, A_WIDTH + LANES), jnp.float32)
    return pl.pallas_call(
        _proj_kernel,
        grid=(n_items + 1,),
        in_specs=[
            pl.BlockSpec((1, tm, d), x_index),
            _const_spec((1, d)),
            _const_spec((d, n_cols)),
            _const_spec((1, LANES)),
            _const_spec((1, LANES)),
            tab_spec, tab_spec, tab_spec,
        ],
        out_specs=[pair_spec(N_PAIRS, 1), pair_spec(1, 1),
                   pl.BlockSpec((1, 1, LANES, tm), vt_index),
                   pair_spec(N_PAIRS, 0), pair_spec(N_PAIRS, 0), pair_spec(N_PAIRS, 0)],
        out_shape=[pair(N_PAIRS), pair(1),
                   jax.ShapeDtypeStruct((b, 1, LANES, s), jnp.bfloat16),
                   pair(N_PAIRS), pair(N_PAIRS), pair(N_PAIRS)],
        scratch_shapes=[raw, raw],
        compiler_params=pltpu.CompilerParams(
            dimension_semantics=("arbitrary",), vmem_limit_bytes=VMEM_LIMIT),
        name="in_projection",
    )(x, norm_g, w_ext, q_g, k_g, cos, sa, sb)


def _exp2_rows(s_ref, m_ref, r0, r1):
    blocks = []
    for a in range(r0, r1, SOFTMAX_ROWS):
        rows = slice(a, a + SOFTMAX_ROWS)
        s = s_ref[rows]
        m = jnp.tile(m_ref[rows], (1, s.shape[1] // LANES))
        blocks.append(jnp.exp2(s - m).astype(jnp.bfloat16))
    return jnp.concatenate(blocks, axis=0)


def _row_max(s_ref, m_ref, r0, r1):
    for a in range(r0, r1, SOFTMAX_ROWS):
        rows = slice(a, a + SOFTMAX_ROWS)
        m = jnp.max(s_ref[rows], axis=-1, keepdims=True)
        m_ref[rows] = jnp.broadcast_to(m, (SOFTMAX_ROWS, LANES))


def _with_ones(v):
    return jnp.concatenate([v, jnp.ones_like(v)], axis=1)


def _normalise(res):
    return res[:, :LANES] / res[:, LANES:]


def _global_attn_kernel(q_ref, k_ref, vt_ref, o_ref, s_a, s_b, m_a, m_b):
    g = pl.program_id(0)
    tq = q_ref.shape[2]
    n_keys = k_ref.shape[2]
    lo = lax.broadcasted_iota(jnp.int32, (1, LANES), 1) < HEAD_DIM
    top = lax.broadcasted_iota(jnp.int32, (LANES, 1), 0) < HEAD_DIM
    sub = 8

    @pl.when(g == 0)
    def _():
        s_b[...] = jnp.zeros_like(s_b)
        m_b[...] = jnp.zeros_like(m_b)

    def step(s_w, m_w, s_r, m_r):
        q = q_ref[0, 0]
        zero = jnp.zeros_like(q)
        qm = jnp.concatenate([jnp.where(lo, q, zero), jnp.where(lo, zero, q)], axis=0)
        vt = vt_ref[0, 0]
        one = jnp.ones_like(vt)
        vt_lo = jnp.where(top, vt, one)
        vt_hi = jnp.where(top, one, vt)
        m_all = jnp.tile(m_r[...], (SOFTMAX_ROWS // sub, 1))
        r_lo = r_hi = None
        pending = []
        for i in range(2 * tq // ATTN_TILE):
            cols = slice(i * ATTN_TILE, (i + 1) * ATTN_TILE)
            s_w[:, cols] = lax.dot_general(k_ref[0, 0], qm[cols], _NT,
                                           preferred_element_type=jnp.float32)
            for c in range(cols.start, cols.stop, LANES):
                m = jnp.max(s_w[:, c:c + LANES], axis=0, keepdims=True)
                m_w[:, c:c + LANES] = jnp.broadcast_to(m, (sub, LANES))

            pending += [jnp.exp2(s_r[a:a + SOFTMAX_ROWS, 0:2 * tq] - m_all).astype(jnp.bfloat16)
                        for a in range(i * ATTN_TILE, (i + 1) * ATTN_TILE, SOFTMAX_ROWS)]
            if (i + 1) % PV_TILES == 0:
                keys = slice((i + 1 - PV_TILES) * ATTN_TILE, (i + 1) * ATTN_TILE)
                pt = jnp.concatenate(pending, axis=0)
                pending = []
                d_lo = jnp.dot(vt_lo[:, keys], pt[:, :tq], preferred_element_type=jnp.float32)
                d_hi = jnp.dot(vt_hi[:, keys], pt[:, tq:], preferred_element_type=jnp.float32)
                r_lo = d_lo if r_lo is None else r_lo + d_lo
                r_hi = d_hi if r_hi is None else r_hi + d_hi
        ot = jnp.concatenate([r_lo[:HEAD_DIM] / r_lo[HEAD_DIM:],
                              r_hi[HEAD_DIM:] / r_hi[:HEAD_DIM]], axis=0)
        o_ref[0, 0] = ot.T.astype(o_ref.dtype)

    @pl.when(g % 2 == 0)
    def _():
        step(s_a, m_a, s_b, m_b)

    @pl.when(g % 2 == 1)
    def _():
        step(s_b, m_b, s_a, m_a)


def _global_attention(qa, ka, va):
    b, n_pairs, s, _ = qa.shape
    tq = ATTN_TQ
    n_chunks = s // tq
    n_items = b * n_pairs * n_chunks

    def item(g, lag):
        i = jnp.clip(g - lag, 0, n_items - 1)
        return i // (n_pairs * n_chunks), (i // n_chunks) % n_pairs, i % n_chunks

    def q_map(lag):
        def index(g):
            bi, p, c = item(g, lag)
            return bi, p, c, 0
        return index

    def kv_map(lag):
        def index(g):
            bi, _, _ = item(g, lag)
            return bi, 0, 0, 0
        return index

    return pl.pallas_call(
        _global_attn_kernel,
        grid=(n_items + 1,),
        in_specs=[pl.BlockSpec((1, 1, tq, LANES), q_map(0)),
                  pl.BlockSpec((1, 1, s, LANES), kv_map(0)),
                  pl.BlockSpec((1, 1, LANES, s), kv_map(1))],
        out_specs=pl.BlockSpec((1, 1, tq, LANES), q_map(1)),
        out_shape=jax.ShapeDtypeStruct(qa.shape, jnp.bfloat16),
        scratch_shapes=[pltpu.VMEM((s, 2 * tq + LANES), jnp.float32),
                        pltpu.VMEM((s, 2 * tq + LANES), jnp.float32),
                        pltpu.VMEM((8, 2 * tq), jnp.float32),
                        pltpu.VMEM((8, 2 * tq), jnp.float32)],
        compiler_params=pltpu.CompilerParams(
            dimension_semantics=("arbitrary",), vmem_limit_bytes=VMEM_LIMIT),
        name="global_attention",
    )(qa, ka, va)


def _nat_kernel(q_ref, k_ref, v_ref, bias_ref, o_ref, s_a, s_b, m_a, m_b, *, rows):
    g = pl.program_id(0)
    lo = lax.broadcasted_iota(jnp.int32, (1, LANES), 1) < HEAD_DIM
    stacked = 2 * GRID_W

    @pl.when(g == 0)
    def _():
        s_b[...] = jnp.zeros_like(s_b)
        m_b[...] = jnp.zeros_like(m_b)

    def band_start(r):
        return min(max(r - WIN_ROWS // 2, 0), rows - WIN_ROWS)

    def step(s_w, m_w, s_r, m_r):
        q = q_ref[0, 0]
        zero = jnp.zeros_like(q)
        q_lo = jnp.where(lo, q, zero)
        q_hi = jnp.where(lo, zero, q)
        for r in range(rows):
            rs = band_start(r)
            tok = slice(r * GRID_W, (r + 1) * GRID_W)
            qs = jnp.concatenate([q_lo[tok], q_hi[tok]], axis=0)
            k = k_ref[0, 0, rs * GRID_W:rs * GRID_W + BAND, :]
            s = lax.dot_general(qs, k, _NT, preferred_element_type=jnp.float32)
            s_w[r * stacked:(r + 1) * stacked] = s + bias_ref[0, rs - r + WIN_ROWS - 1]
        _row_max(s_w, m_w, 0, rows * stacked)

        v1 = _with_ones(v_ref[0, 0])
        for r in range(rows):
            rs = band_start(r)
            p = _exp2_rows(s_r, m_r, r * stacked, (r + 1) * stacked)
            o = _normalise(jnp.dot(p, v1[rs * GRID_W:rs * GRID_W + BAND],
                                   preferred_element_type=jnp.float32))
            o_ref[0, 0, r * GRID_W:(r + 1) * GRID_W, :] = jnp.where(
                lo, o[:GRID_W], o[GRID_W:]).astype(o_ref.dtype)

    @pl.when(g % 2 == 0)
    def _():
        step(s_a, m_a, s_b, m_b)

    @pl.when(g % 2 == 1)
    def _():
        step(s_b, m_b, s_a, m_a)


def _neighbourhood_attention(qb, kb, vb, bias):
    b, n_pairs, s, _ = qb.shape
    rows = s // GRID_W
    n_items = n_pairs * b

    def seq_map(lag):
        def index(g):
            i = jnp.clip(g - lag, 0, n_items - 1)
            return i % b, i // b, 0, 0
        return index

    seq = lambda lag: pl.BlockSpec((1, 1, s, LANES), seq_map(lag))
    bias_spec = pl.BlockSpec((1, WIN_ROWS, 2 * GRID_W, BAND),
                             lambda g: (jnp.minimum(g, n_items - 1) // b, 0, 0, 0))
    score_rows = rows * 2 * GRID_W
    return pl.pallas_call(
        functools.partial(_nat_kernel, rows=rows),
        grid=(n_items + 1,),
        in_specs=[seq(0), seq(0), seq(1), bias_spec],
        out_specs=seq(1),
        out_shape=jax.ShapeDtypeStruct(qb.shape, jnp.bfloat16),
        scratch_shapes=[pltpu.VMEM((score_rows, BAND), jnp.float32),
                        pltpu.VMEM((score_rows, BAND), jnp.float32),
                        pltpu.VMEM((score_rows, LANES), jnp.float32),
                        pltpu.VMEM((score_rows, LANES), jnp.float32)],
        compiler_params=pltpu.CompilerParams(
            dimension_semantics=("arbitrary",), vmem_limit_bytes=VMEM_LIMIT),
        name="neighbourhood_attention",
    )(qb, kb, vb, bias)


def _mlp_kernel(x_ref, ya_ref, yb_ref, ga_ref, gb_ref, wo_ref, gm_ref, wu_ref, wd_ref,
                gf_ref, o_ref):
    ya = jnp.concatenate([ya_ref[0, p] for p in range(N_PAIRS)], axis=-1)
    yb = jnp.concatenate([yb_ref[0, p] for p in range(N_PAIRS)], axis=-1)
    y = jnp.concatenate([_rms(ya.astype(jnp.float32), ga_ref[...]),
                         _rms(yb.astype(jnp.float32), gb_ref[...])], axis=-1)
    x1 = x_ref[0] + jnp.dot(y.astype(jnp.bfloat16), wo_ref[...],
                            preferred_element_type=jnp.float32)
    h = _rms(x1, gm_ref[...]).astype(jnp.bfloat16)
    acc = x1
    for c in range(D_FF // FF_CHUNK):
        sl = slice(c * FF_CHUNK, (c + 1) * FF_CHUNK)
        u = jnp.dot(h, wu_ref[:, sl], preferred_element_type=jnp.float32)
        u = jnp.square(jnp.maximum(u, 0.0)).astype(jnp.bfloat16)
        acc = acc + jnp.dot(u, wd_ref[sl, :], preferred_element_type=jnp.float32)
    o_ref[0] = _rms(acc, gf_ref[...])


def _out_mlp(x, ya, yb, ga, gb, w_out, gm, w_up, w_down, gf):
    b, s, d = x.shape
    tm = MLP_TM
    x_spec = pl.BlockSpec((1, tm, d), lambda bi, si: (bi, si, 0))
    y_spec = pl.BlockSpec((1, N_PAIRS, tm, LANES), lambda bi, si: (bi, 0, si, 0))
    return pl.pallas_call(
        _mlp_kernel,
        grid=(b, s // tm),
        in_specs=[x_spec, y_spec, y_spec,
                  _const_spec((1, A_WIDTH)), _const_spec((1, B_WIDTH)),
                  _const_spec(w_out.shape), _const_spec((1, d)),
                  _const_spec(w_up.shape), _const_spec(w_down.shape),
                  _const_spec((1, d))],
        out_specs=x_spec,
        out_shape=jax.ShapeDtypeStruct(x.shape, jnp.float32),
        compiler_params=pltpu.CompilerParams(
            dimension_semantics=("parallel", "parallel"), vmem_limit_bytes=VMEM_LIMIT),
        name="out_proj_mlp",
    )(x, ya, yb, ga, gb, w_out, gm, w_up, w_down, gf)


def _rope_tables(seq_len):
    t = jnp.arange(seq_len, dtype=jnp.int32)
    freqs = ROPE_THETA ** (-jnp.arange(ROPE_HALF, dtype=jnp.float32) / ROPE_HALF)
    zeros = jnp.zeros((seq_len, ROPE_HALF), jnp.float32)
    cos_parts, sa_parts, sb_parts = [], [], []
    for pos in (t // GRID_W, t % GRID_W):
        ang = pos.astype(jnp.float32)[:, None] * freqs[None, :]
        c, sn = jnp.cos(ang), jnp.sin(ang)
        cos_parts += [c, c]
        sa_parts += [-sn, zeros]
        sb_parts += [zeros, sn]
    head = lambda parts: jnp.tile(jnp.concatenate(parts, axis=-1), (1, LANES // HEAD_DIM))
    return head(cos_parts), head(sa_parts), head(sb_parts)


def _pair_heads(a, axis):
    shape = a.shape
    a = a.reshape(shape[:axis] + (A_KV_HEADS, A_GROUP, HEAD_DIM) + shape[axis + 1:])
    return jnp.swapaxes(a, axis, axis + 1).reshape(shape)


def _paired_w_in(w_in):
    qa = _pair_heads(w_in[:, :A_WIDTH], 1)
    return jnp.concatenate([qa, w_in[:, A_WIDTH:]], axis=1).astype(jnp.bfloat16)


def _encoder(x, p):
    qa, ka, va, qb, kb, vb = _in_projection(
        x, p["norm_attn_g"], p["w_in"], p["q_g"], p["k_g"], *p["rope"])
    ya = _global_attention(qa, ka, va)
    yb = _neighbourhood_attention(qb, kb, vb, p["bias"])
    return _out_mlp(x, ya, yb, p["ga"], p["gb"], p["w_out"], p["gm"], p["w_up"],
                    p["w_down"], p["gf"])


def kernel(x_prompt, x_sample, norm_attn_g, w_in, q_norm_g, k_norm_g, nat_rel_bias,
           out_norm_a_g, out_norm_b_g, w_out, norm_mlp_g, w_up, w_down, final_norm_g):
    assert norm_attn_g.shape[0] == 1, "single trunk layer"
    assert x_prompt.shape[1] == x_sample.shape[1]
    row = lambda g: g.reshape(1, -1).astype(jnp.float32)
    pair_gain = lambda g: jnp.tile(row(g), (1, LANES // HEAD_DIM))
    params = {
        "norm_attn_g": row(norm_attn_g[0]),
        "w_in": _paired_w_in(w_in[0]),
        "q_g": pair_gain(q_norm_g[0]),
        "k_g": pair_gain(k_norm_g[0]),
        "rope": _rope_tables(x_prompt.shape[1]),
        "bias": _bias_tables(nat_rel_bias[0]),
        "ga": row(_pair_heads(out_norm_a_g[0], 0)),
        "gb": row(out_norm_b_g[0]),
        "w_out": jnp.concatenate([_pair_heads(w_out[0, :A_WIDTH], 0), w_out[0, A_WIDTH:]],
                                 axis=0).astype(jnp.bfloat16),
        "gm": row(norm_mlp_g[0]),
        "w_up": w_up[0].astype(jnp.bfloat16),
        "w_down": w_down[0].astype(jnp.bfloat16),
        "gf": row(final_norm_g),
    }
    return (_encoder(x_prompt, params), _encoder(x_sample, params))
```

```python
import functools

import jax
import jax.numpy as jnp
from jax import lax
from jax.experimental import pallas as pl
from jax.experimental.pallas import tpu as pltpu

D_MODEL = 1024
HEAD_DIM = 64
LANES = 128
A_WIDTH = 512
B_WIDTH = 512
A_KV_HEADS = 2
A_GROUP = 4
B_HEADS = 8
GRID_W = 64
WIN_ROWS = 8
WIN_COLS = 16
ROPE_THETA = 10000.0
ROPE_HALF = 16
D_FF = 4 * D_MODEL
EPS = 1e-6
SCALE = HEAD_DIM ** -0.5
MASKED = -1e30

N_PAIRS = A_WIDTH // LANES
BAND = WIN_ROWS * GRID_W
PROJ_TM = 1024
ATTN_TILE = 256
SOFTMAX_ROWS = 16
LOG2E = 1.4426950408889634
MLP_TM = 1024
FF_CHUNK = 1024
VMEM_LIMIT = 56 * 1024 * 1024

_NT = (((1,), (1,)), ((), ()))


def _const_spec(shape):
    zeros = (0,) * len(shape)
    return pl.BlockSpec(shape, lambda *_: zeros, pipeline_mode=pl.Buffered(1))


def _rms(x, g):
    return x * lax.rsqrt(jnp.mean(x * x, axis=-1, keepdims=True) + EPS) * g


def _bias_kernel(rb_ref, out_ref):
    h = pl.program_id(0)
    n_dr = 2 * WIN_ROWS - 1
    n_dc = 2 * WIN_COLS - 1
    c = lax.broadcasted_iota(jnp.int32, (GRID_W, LANES), 0)
    lane = lax.broadcasted_iota(jnp.int32, (GRID_W, LANES), 1)
    kc = lane % GRID_W
    first = lane < GRID_W
    d = kc - c + (WIN_COLS - 1)
    cs = jnp.clip(c - WIN_COLS // 2, 0, GRID_W - WIN_COLS)
    valid = (kc >= cs) & (kc < cs + WIN_COLS)
    hits = [d == dd for dd in range(n_dc)]
    base = h * (n_dr * n_dc)
    pair_tiles = []
    for dr in range(n_dr - 1):
        acc = jnp.zeros((GRID_W, LANES), jnp.float32)
        for dd in range(n_dc):
            v0 = rb_ref[base + dr * n_dc + dd]
            v1 = rb_ref[base + (dr + 1) * n_dc + dd]
            acc = jnp.where(hits[dd], jnp.where(first, v0, v1), acc)
        pair_tiles.append(jnp.where(valid, acc * LOG2E, MASKED))
    for a in range(WIN_ROWS):
        for jj in range(WIN_ROWS // 2):
            out_ref[0, a, :, jj * LANES:(jj + 1) * LANES] = pair_tiles[a + 2 * jj]


def _bias_tables(rel_bias):
    flat = rel_bias.reshape(-1).astype(jnp.float32)
    return pl.pallas_call(
        _bias_kernel,
        grid=(B_HEADS,),
        in_specs=[pl.BlockSpec(memory_space=pltpu.SMEM)],
        out_specs=pl.BlockSpec((1, WIN_ROWS, GRID_W, BAND), lambda h: (h // 2, 0, h % 2, 0)),
        out_shape=jax.ShapeDtypeStruct((N_PAIRS, WIN_ROWS, 2 * GRID_W, BAND), jnp.float32),
        name="nat_bias_tables",
    )(flat)


def _proj_kernel(x_ref, g_ref, w_ref, qg_ref, kg_ref, cos_ref, sa_ref, sb_ref,
                 qa_ref, ka_ref, va_ref, qb_ref, kb_ref, vb_ref, raw_a, raw_b):
    g = pl.program_id(0)
    lo = lax.broadcasted_iota(jnp.int32, (1, LANES), 1) < HEAD_DIM
    qk_width = A_WIDTH + LANES

    @pl.when(g == 0)
    def _():
        raw_b[...] = jnp.zeros_like(raw_b)

    def norm_rope(t, gain):
        sq = t * t
        s_lo = jnp.sum(jnp.where(lo, sq, 0.0), axis=-1, keepdims=True)
        s_hi = jnp.sum(jnp.where(lo, 0.0, sq), axis=-1, keepdims=True)
        ms = jnp.where(lo, s_lo, s_hi) * (1.0 / HEAD_DIM)
        y = t * lax.rsqrt(ms + EPS) * gain
        return (y * cos_ref[...] + pltpu.roll(y, LANES - ROPE_HALF, 1) * sa_ref[...]
                + pltpu.roll(y, ROPE_HALF, 1) * sb_ref[...])

    def step(raw_w, raw_r):
        h = _rms(x_ref[0], g_ref[...]).astype(jnp.bfloat16)

        def proj(col, width):
            return jnp.dot(h, w_ref[:, col:col + width], preferred_element_type=jnp.float32)

        raw_w[:, :A_WIDTH] = proj(0, A_WIDTH)
        kva = proj(A_WIDTH, 2 * LANES)
        raw_w[:, A_WIDTH:qk_width] = kva[:, :LANES]
        va_ref[0, 0] = kva[:, LANES:].T.astype(jnp.bfloat16)
        col = A_WIDTH + 2 * LANES
        qb = proj(col, B_WIDTH)
        kb = proj(col + B_WIDTH, B_WIDTH)
        vb = proj(col + 2 * B_WIDTH, B_WIDTH)
        for p in range(N_PAIRS):
            sl = slice(p * LANES, (p + 1) * LANES)
            qb_ref[0, p] = (qb[:, sl] * (SCALE * LOG2E)).astype(jnp.bfloat16)
            kb_ref[0, p] = kb[:, sl].astype(jnp.bfloat16)
            vb_ref[0, p] = vb[:, sl].astype(jnp.bfloat16)

        for p in range(N_PAIRS):
            t = norm_rope(raw_r[:, p * LANES:(p + 1) * LANES], qg_ref[...])
            qa_ref[0, p] = (t * (SCALE * LOG2E)).astype(jnp.bfloat16)
        ka_ref[0, 0] = norm_rope(raw_r[:, A_WIDTH:qk_width], kg_ref[...]).astype(jnp.bfloat16)

    @pl.when(g % 2 == 0)
    def _():
        step(raw_a, raw_b)

    @pl.when(g % 2 == 1)
    def _():
        step(raw_b, raw_a)


def _in_projection(x, norm_g, w_ext, q_g, k_g, cos, sa, sb):
    b, s, d = x.shape
    tm = PROJ_TM
    n_tiles = s // tm
    n_items = b * n_tiles
    n_cols = w_ext.shape[1]

    def item(g, lag):
        i = jnp.clip(g - lag, 0, n_items - 1)
        return i // n_tiles, i % n_tiles

    def pair_spec(n, lag):
        def index(g):
            bi, si = item(g, lag)
            return bi, 0, si, 0
        return pl.BlockSpec((1, n, tm, LANES), index)

    def x_index(g):
        bi, si = item(g, 0)
        return bi, si, 0

    def vt_index(g):
        bi, si = item(g, 0)
        return bi, 0, 0, si

    pair = lambda n: jax.ShapeDtypeStruct((b, n, s, LANES), jnp.bfloat16)
    tab_spec = pl.BlockSpec((tm, LANES), lambda g: (item(g, 1)[1], 0))
    raw = pltpu.VMEM((tm, A_WIDTH + LANES), jnp.float32)
    return pl.pallas_call(
        _proj_kernel,
        grid=(n_items + 1,),
        in_specs=[
            pl.BlockSpec((1, tm, d), x_index),
            _const_spec((1, d)),
            _const_spec((d, n_cols)),
            _const_spec((1, LANES)),
            _const_spec((1, LANES)),
            tab_spec, tab_spec, tab_spec,
        ],
        out_specs=[pair_spec(N_PAIRS, 1), pair_spec(1, 1),
                   pl.BlockSpec((1, 1, LANES, tm), vt_index),
                   pair_spec(N_PAIRS, 0), pair_spec(N_PAIRS, 0), pair_spec(N_PAIRS, 0)],
        out_shape=[pair(N_PAIRS), pair(1),
                   jax.ShapeDtypeStruct((b, 1, LANES, s), jnp.bfloat16),
                   pair(N_PAIRS), pair(N_PAIRS), pair(N_PAIRS)],
        scratch_shapes=[raw, raw],
        compiler_params=pltpu.CompilerParams(
            dimension_semantics=("arbitrary",), vmem_limit_bytes=VMEM_LIMIT),
        name="in_projection",
    )(x, norm_g, w_ext, q_g, k_g, cos, sa, sb)


def _exp2_rows(s_ref, m_ref, r0, r1):
    blocks = []
    for a in range(r0, r1, SOFTMAX_ROWS):
        rows = slice(a, a + SOFTMAX_ROWS)
        s = s_ref[rows]
        m = jnp.tile(m_ref[rows], (1, s.shape[1] // LANES))
        blocks.append(jnp.exp2(s - m).astype(jnp.bfloat16))
    return jnp.concatenate(blocks, axis=0)


def _row_max(s_ref, m_ref, r0, r1):
    for a in range(r0, r1, SOFTMAX_ROWS):
        rows = slice(a, a + SOFTMAX_ROWS)
        m = jnp.max(s_ref[rows], axis=-1, keepdims=True)
        m_ref[rows] = jnp.broadcast_to(m, (SOFTMAX_ROWS, LANES))


def _with_ones(v):
    return jnp.concatenate([v, jnp.ones_like(v)], axis=1)


def _normalise(res):
    return res[:, :LANES] / res[:, LANES:]


def _global_attn_kernel(q_ref, k_ref, qn_ref, kn_ref, vt_ref, o_ref, s_0, s_1, m_0, m_1):
    g = pl.program_id(0)
    n_q = q_ref.shape[2]
    n_tiles = 2 * (n_q // ATTN_TILE)
    lo = lax.broadcasted_iota(jnp.int32, (1, LANES), 1) < HEAD_DIM
    top = lax.broadcasted_iota(jnp.int32, (LANES, 1), 0) < HEAD_DIM
    sub = 8
    bufs = ((s_0, m_0), (s_1, m_1))

    def scores(q_blk, k_blk, t, s_w, m_w):
        rows = slice((t // 2) * ATTN_TILE, (t // 2 + 1) * ATTN_TILE)
        q = q_blk[0, 0, rows, :]
        zero = jnp.zeros_like(q)
        qm = jnp.where(lo, q, zero) if t % 2 == 0 else jnp.where(lo, zero, q)
        half = k_blk.shape[2] // 2
        for keys in (slice(0, half), slice(half, 2 * half)):
            s_w[keys] = lax.dot_general(k_blk[0, 0, keys, :], qm, _NT,
                                        preferred_element_type=jnp.float32)
        for c in range(0, ATTN_TILE, LANES):
            m = jnp.max(s_w[:, c:c + LANES], axis=0, keepdims=True)
            m_w[:, c:c + LANES] = jnp.broadcast_to(m, (sub, LANES))

    @pl.when(g == 0)
    def _():
        scores(q_ref, k_ref, 0, s_0, m_0)

    vt = vt_ref[0, 0]
    one = jnp.ones_like(vt)
    vt_ones = (jnp.where(top, vt, one), jnp.where(top, one, vt))
    n_keys = vt.shape[1]
    o_lo = None
    for t in range(n_tiles):
        s_w, m_w = bufs[(t + 1) % 2]
        if t + 1 < n_tiles:
            scores(q_ref, k_ref, t + 1, s_w, m_w)
        else:
            scores(qn_ref, kn_ref, 0, s_w, m_w)
        s_r, m_r = bufs[t % 2]
        m_all = jnp.tile(m_r[...], (SOFTMAX_ROWS // sub, 1))
        pt = jnp.concatenate(
            [jnp.exp2(s_r[a:a + SOFTMAX_ROWS] - m_all).astype(jnp.bfloat16)
             for a in range(0, n_keys, SOFTMAX_ROWS)], axis=0)
        d = jnp.dot(vt_ones[t % 2], pt, preferred_element_type=jnp.float32)
        if t % 2 == 0:
            o_lo = d[:HEAD_DIM] / d[HEAD_DIM:]
        else:
            ot = jnp.concatenate([o_lo, d[HEAD_DIM:] / d[:HEAD_DIM]], axis=0)
            rows = slice((t // 2) * ATTN_TILE, (t // 2 + 1) * ATTN_TILE)
            o_ref[0, 0, rows, :] = ot.T.astype(o_ref.dtype)


def _global_attention(qa, ka, va):
    b, n_pairs, s, _ = qa.shape
    n_items = b * n_pairs

    def q_map(ahead):
        def index(g):
            i = jnp.minimum(g + ahead, n_items - 1)
            return i // n_pairs, i % n_pairs, 0, 0
        return index

    def kv_map(ahead):
        def index(g):
            return jnp.minimum(g + ahead, n_items - 1) // n_pairs, 0, 0, 0
        return index

    seq = lambda index: pl.BlockSpec((1, 1, s, LANES), index)
    return pl.pallas_call(
        _global_attn_kernel,
        grid=(n_items,),
        in_specs=[seq(q_map(0)), seq(kv_map(0)), seq(q_map(1)), seq(kv_map(1)),
                  pl.BlockSpec((1, 1, LANES, s), kv_map(0))],
        out_specs=seq(q_map(0)),
        out_shape=jax.ShapeDtypeStruct(qa.shape, jnp.bfloat16),
        scratch_shapes=[pltpu.VMEM((s, ATTN_TILE), jnp.float32),
                        pltpu.VMEM((s, ATTN_TILE), jnp.float32),
                        pltpu.VMEM((8, ATTN_TILE), jnp.float32),
                        pltpu.VMEM((8, ATTN_TILE), jnp.float32)],
        compiler_params=pltpu.CompilerParams(
            dimension_semantics=("arbitrary",), vmem_limit_bytes=VMEM_LIMIT),
        name="global_attention",
    )(qa, ka, qa, ka, va)


def _nat_kernel(q_ref, k_ref, v_ref, bias_ref, o_ref, s_a, s_b, m_a, m_b, *, rows):
    g = pl.program_id(0)
    lo = lax.broadcasted_iota(jnp.int32, (1, LANES), 1) < HEAD_DIM
    stacked = 2 * GRID_W

    @pl.when(g == 0)
    def _():
        s_b[...] = jnp.zeros_like(s_b)
        m_b[...] = jnp.zeros_like(m_b)

    def band_start(r):
        return min(max(r - WIN_ROWS // 2, 0), rows - WIN_ROWS)

    def step(s_w, m_w, s_r, m_r):
        q = q_ref[0, 0]
        zero = jnp.zeros_like(q)
        q_lo = jnp.where(lo, q, zero)
        q_hi = jnp.where(lo, zero, q)
        for r in range(rows):
            rs = band_start(r)
            tok = slice(r * GRID_W, (r + 1) * GRID_W)
            qs = jnp.concatenate([q_lo[tok], q_hi[tok]], axis=0)
            k = k_ref[0, 0, rs * GRID_W:rs * GRID_W + BAND, :]
            s = lax.dot_general(qs, k, _NT, preferred_element_type=jnp.float32)
            s_w[r * stacked:(r + 1) * stacked] = s + bias_ref[0, rs - r + WIN_ROWS - 1]
        _row_max(s_w, m_w, 0, rows * stacked)

        v1 = _with_ones(v_ref[0, 0])
        for r in range(rows):
            rs = band_start(r)
            p = _exp2_rows(s_r, m_r, r * stacked, (r + 1) * stacked)
            o = _normalise(jnp.dot(p, v1[rs * GRID_W:rs * GRID_W + BAND],
                                   preferred_element_type=jnp.float32))
            o_ref[0, 0, r * GRID_W:(r + 1) * GRID_W, :] = jnp.where(
                lo, o[:GRID_W], o[GRID_W:]).astype(o_ref.dtype)

    @pl.when(g % 2 == 0)
    def _():
        step(s_a, m_a, s_b, m_b)

    @pl.when(g % 2 == 1)
    def _():
        step(s_b, m_b, s_a, m_a)


def _neighbourhood_attention(qb, kb, vb, bias):
    b, n_pairs, s, _ = qb.shape
    rows = s // GRID_W
    n_items = n_pairs * b

    def seq_map(lag):
        def index(g):
            i = jnp.clip(g - lag, 0, n_items - 1)
            return i % b, i // b, 0, 0
        return index

    seq = lambda lag: pl.BlockSpec((1, 1, s, LANES), seq_map(lag))
    bias_spec = pl.BlockSpec((1, WIN_ROWS, 2 * GRID_W, BAND),
                             lambda g: (jnp.minimum(g, n_items - 1) // b, 0, 0, 0))
    score_rows = rows * 2 * GRID_W
    return pl.pallas_call(
        functools.partial(_nat_kernel, rows=rows),
        grid=(n_items + 1,),
        in_specs=[seq(0), seq(0), seq(1), bias_spec],
        out_specs=seq(1),
        out_shape=jax.ShapeDtypeStruct(qb.shape, jnp.bfloat16),
        scratch_shapes=[pltpu.VMEM((score_rows, BAND), jnp.float32),
                        pltpu.VMEM((score_rows, BAND), jnp.float32),
                        pltpu.VMEM((score_rows, LANES), jnp.float32),
                        pltpu.VMEM((score_rows, LANES), jnp.float32)],
        compiler_params=pltpu.CompilerParams(
            dimension_semantics=("arbitrary",), vmem_limit_bytes=VMEM_LIMIT),
        name="neighbourhood_attention",
    )(qb, kb, vb, bias)


def _mlp_kernel(x_ref, ya_ref, yb_ref, ga_ref, gb_ref, wo_ref, gm_ref, wu_ref, wd_ref,
                gf_ref, o_ref):
    ya = jnp.concatenate([ya_ref[0, p] for p in range(N_PAIRS)], axis=-1)
    yb = jnp.concatenate([yb_ref[0, p] for p in range(N_PAIRS)], axis=-1)
    y = jnp.concatenate([_rms(ya.astype(jnp.float32), ga_ref[...]),
                         _rms(yb.astype(jnp.float32), gb_ref[...])], axis=-1)
    x1 = x_ref[0] + jnp.dot(y.astype(jnp.bfloat16), wo_ref[...],
                            preferred_element_type=jnp.float32)
    h = _rms(x1, gm_ref[...]).astype(jnp.bfloat16)
    acc = x1
    for c in range(D_FF // FF_CHUNK):
        sl = slice(c * FF_CHUNK, (c + 1) * FF_CHUNK)
        u = jnp.dot(h, wu_ref[:, sl], preferred_element_type=jnp.float32)
        u = jnp.square(jnp.maximum(u, 0.0)).astype(jnp.bfloat16)
        acc = acc + jnp.dot(u, wd_ref[sl, :], preferred_element_type=jnp.float32)
    o_ref[0] = _rms(acc, gf_ref[...])


def _out_mlp(x, ya, yb, ga, gb, w_out, gm, w_up, w_down, gf):
    b, s, d = x.shape
    tm = MLP_TM
    x_spec = pl.BlockSpec((1, tm, d), lambda bi, si: (bi, si, 0))
    y_spec = pl.BlockSpec((1, N_PAIRS, tm, LANES), lambda bi, si: (bi, 0, si, 0))
    return pl.pallas_call(
        _mlp_kernel,
        grid=(b, s // tm),
        in_specs=[x_spec, y_spec, y_spec,
                  _const_spec((1, A_WIDTH)), _const_spec((1, B_WIDTH)),
                  _const_spec(w_out.shape), _const_spec((1, d)),
                  _const_spec(w_up.shape), _const_spec(w_down.shape),
                  _const_spec((1, d))],
        out_specs=x_spec,
        out_shape=jax.ShapeDtypeStruct(x.shape, jnp.float32),
        compiler_params=pltpu.CompilerParams(
            dimension_semantics=("parallel", "parallel"), vmem_limit_bytes=VMEM_LIMIT),
        name="out_proj_mlp",
    )(x, ya, yb, ga, gb, w_out, gm, w_up, w_down, gf)


def _rope_tables(seq_len):
    t = jnp.arange(seq_len, dtype=jnp.int32)
    freqs = ROPE_THETA ** (-jnp.arange(ROPE_HALF, dtype=jnp.float32) / ROPE_HALF)
    zeros = jnp.zeros((seq_len, ROPE_HALF), jnp.float32)
    cos_parts, sa_parts, sb_parts = [], [], []
    for pos in (t // GRID_W, t % GRID_W):
        ang = pos.astype(jnp.float32)[:, None] * freqs[None, :]
        c, sn = jnp.cos(ang), jnp.sin(ang)
        cos_parts += [c, c]
        sa_parts += [-sn, zeros]
        sb_parts += [zeros, sn]
    head = lambda parts: jnp.tile(jnp.concatenate(parts, axis=-1), (1, LANES // HEAD_DIM))
    return head(cos_parts), head(sa_parts), head(sb_parts)


def _pair_heads(a, axis):
    shape = a.shape
    a = a.reshape(shape[:axis] + (A_KV_HEADS, A_GROUP, HEAD_DIM) + shape[axis + 1:])
    return jnp.swapaxes(a, axis, axis + 1).reshape(shape)


def _paired_w_in(w_in):
    qa = _pair_heads(w_in[:, :A_WIDTH], 1)
    return jnp.concatenate([qa, w_in[:, A_WIDTH:]], axis=1).astype(jnp.bfloat16)


def _encoder(x, p):
    qa, ka, va, qb, kb, vb = _in_projection(
        x, p["norm_attn_g"], p["w_in"], p["q_g"], p["k_g"], *p["rope"])
    ya = _global_attention(qa, ka, va)
    yb = _neighbourhood_attention(qb, kb, vb, p["bias"])
    return _out_mlp(x, ya, yb, p["ga"], p["gb"], p["w_out"], p["gm"], p["w_up"],
                    p["w_down"], p["gf"])


def kernel(x_prompt, x_sample, norm_attn_g, w_in, q_norm_g, k_norm_g, nat_rel_bias,
           out_norm_a_g, out_norm_b_g, w_out, norm_mlp_g, w_up, w_down, final_norm_g):
    assert norm_attn_g.shape[0] == 1, "single trunk layer"
    assert x_prompt.shape[1] == x_sample.shape[1]
    row = lambda g: g.reshape(1, -1).astype(jnp.float32)
    pair_gain = lambda g: jnp.tile(row(g), (1, LANES // HEAD_DIM))
    params = {
        "norm_attn_g": row(norm_attn_g[0]),
        "w_in": _paired_w_in(w_in[0]),
        "q_g": pair_gain(q_norm_g[0]),
        "k_g": pair_gain(k_norm_g[0]),
        "rope": _rope_tables(x_prompt.shape[1]),
        "bias": _bias_tables(nat_rel_bias[0]),
        "ga": row(_pair_heads(out_norm_a_g[0], 0)),
        "gb": row(out_norm_b_g[0]),
        "w_out": jnp.concatenate([_pair_heads(w_out[0, :A_WIDTH], 0), w_out[0, A_WIDTH:]],
                                 axis=0).astype(jnp.bfloat16),
        "gm": row(norm_mlp_g[0]),
        "w_up": w_up[0].astype(jnp.bfloat16),
        "w_down": w_down[0].astype(jnp.bfloat16),
        "gf": row(final_norm_g),
    }
    return (_encoder(x_prompt, params), _encoder(x_sample, params))
```

```python
import functools

import jax
import jax.numpy as jnp
from jax import lax
from jax.experimental import pallas as pl
from jax.experimental.pallas import tpu as pltpu

D_MODEL = 1024
HEAD_DIM = 64
LANES = 128
A_WIDTH = 512
B_WIDTH = 512
A_KV_HEADS = 2
A_GROUP = 4
B_HEADS = 8
GRID_W = 64
WIN_ROWS = 8
WIN_COLS = 16
ROPE_THETA = 10000.0
ROPE_HALF = 16
D_FF = 4 * D_MODEL
EPS = 1e-6
SCALE = HEAD_DIM ** -0.5
MASKED = -1e30

N_PAIRS = A_WIDTH // LANES
BAND = WIN_ROWS * GRID_W
PROJ_TM = 1024
ATTN_TILE = 256
SOFTMAX_ROWS = 16
NAT_BATCH = 4
LOG2E = 1.4426950408889634
MLP_TM = 1024
FF_CHUNK = 1024
VMEM_LIMIT = 56 * 1024 * 1024

_NT = (((1,), (1,)), ((), ()))


def _const_spec(shape):
    zeros = (0,) * len(shape)
    return pl.BlockSpec(shape, lambda *_: zeros, pipeline_mode=pl.Buffered(1))


def _rms(x, g):
    return x * lax.rsqrt(jnp.mean(x * x, axis=-1, keepdims=True) + EPS) * g


def _bias_kernel(rb_ref, out_ref):
    h = pl.program_id(0)
    n_dr = 2 * WIN_ROWS - 1
    n_dc = 2 * WIN_COLS - 1
    c = lax.broadcasted_iota(jnp.int32, (GRID_W, LANES), 0)
    lane = lax.broadcasted_iota(jnp.int32, (GRID_W, LANES), 1)
    kc = lane % GRID_W
    first = lane < GRID_W
    d = kc - c + (WIN_COLS - 1)
    cs = jnp.clip(c - WIN_COLS // 2, 0, GRID_W - WIN_COLS)
    valid = (kc >= cs) & (kc < cs + WIN_COLS)
    hits = [d == dd for dd in range(n_dc)]
    base = h * (n_dr * n_dc)
    tiles = []
    for dr in range(n_dr):
        acc = jnp.zeros((GRID_W, LANES), jnp.float32)
        for dd in range(n_dc):
            acc = jnp.where(hits[dd], rb_ref[base + dr * n_dc + dd], acc)
        tiles.append(jnp.where(valid, acc * LOG2E, MASKED))
    pair_tiles = [jnp.where(first, tiles[dr], tiles[dr + 1])
                  for dr in range(n_dr - 1)]
    for a in range(WIN_ROWS):
        for jj in range(WIN_ROWS // 2):
            out_ref[0, a, :, jj * LANES:(jj + 1) * LANES] = pair_tiles[a + 2 * jj]


def _bias_tables(rel_bias):
    flat = rel_bias.reshape(-1).astype(jnp.float32)
    return pl.pallas_call(
        _bias_kernel,
        grid=(B_HEADS,),
        in_specs=[pl.BlockSpec(memory_space=pltpu.SMEM)],
        out_specs=pl.BlockSpec((1, WIN_ROWS, GRID_W, BAND), lambda h: (h // 2, 0, h % 2, 0)),
        out_shape=jax.ShapeDtypeStruct((N_PAIRS, WIN_ROWS, 2 * GRID_W, BAND), jnp.float32),
        name="nat_bias_tables",
    )(flat)


def _proj_kernel(x_ref, g_ref, w_ref, qg_ref, kg_ref, cos_ref, sa_ref, sb_ref,
                 qa_ref, ka_ref, va_ref, qb_ref, kb_ref, vb_ref, raw_a, raw_b):
    g = pl.program_id(0)
    lo = lax.broadcasted_iota(jnp.int32, (1, LANES), 1) < HEAD_DIM
    qk_width = A_WIDTH + LANES

    @pl.when(g == 0)
    def _():
        raw_b[...] = jnp.zeros_like(raw_b)

    def norm_rope(t, gain):
        sq = t * t
        s_lo = jnp.sum(jnp.where(lo, sq, 0.0), axis=-1, keepdims=True)
        s_hi = jnp.sum(jnp.where(lo, 0.0, sq), axis=-1, keepdims=True)
        ms = jnp.where(lo, s_lo, s_hi) * (1.0 / HEAD_DIM)
        y = t * lax.rsqrt(ms + EPS) * gain
        return (y * cos_ref[...] + pltpu.roll(y, LANES - ROPE_HALF, 1) * sa_ref[...]
                + pltpu.roll(y, ROPE_HALF, 1) * sb_ref[...])

    def step(raw_w, raw_r):
        h = _rms(x_ref[0], g_ref[...]).astype(jnp.bfloat16)

        def proj(col, width):
            return jnp.dot(h, w_ref[:, col:col + width], preferred_element_type=jnp.float32)

        raw_w[:, :A_WIDTH] = proj(0, A_WIDTH)
        kva = proj(A_WIDTH, 2 * LANES)
        raw_w[:, A_WIDTH:qk_width] = kva[:, :LANES]
        va_ref[0, 0] = kva[:, LANES:].T.astype(jnp.bfloat16)
        col = A_WIDTH + 2 * LANES
        qb = proj(col, B_WIDTH)
        kb = proj(col + B_WIDTH, B_WIDTH)
        vb = proj(col + 2 * B_WIDTH, B_WIDTH)
        for p in range(N_PAIRS):
            sl = slice(p * LANES, (p + 1) * LANES)
            qb_ref[0, p] = (qb[:, sl] * (SCALE * LOG2E)).astype(jnp.bfloat16)
            kb_ref[0, p] = kb[:, sl].astype(jnp.bfloat16)
            vb_ref[0, p] = vb[:, sl].astype(jnp.bfloat16)

        for p in range(N_PAIRS):
            t = norm_rope(raw_r[:, p * LANES:(p + 1) * LANES], qg_ref[...])
            qa_ref[0, p] = (t * (SCALE * LOG2E)).astype(jnp.bfloat16)
        ka_ref[0, 0] = norm_rope(raw_r[:, A_WIDTH:qk_width], kg_ref[...]).astype(jnp.bfloat16)

    @pl.when(g % 2 == 0)
    def _():
        step(raw_a, raw_b)

    @pl.when(g % 2 == 1)
    def _():
        step(raw_b, raw_a)


def _in_projection(x, norm_g, w_ext, q_g, k_g, cos, sa, sb):
    b, s, d = x.shape
    tm = PROJ_TM
    n_tiles = s // tm
    n_items = b * n_tiles
    n_cols = w_ext.shape[1]

    def item(g, lag):
        i = jnp.clip(g - lag, 0, n_items - 1)
        return i // n_tiles, i % n_tiles

    def pair_spec(n, lag):
        def index(g):
            bi, si = item(g, lag)
            return bi, 0, si, 0
        return pl.BlockSpec((1, n, tm, LANES), index)

    def x_index(g):
        bi, si = item(g, 0)
        return bi, si, 0

    def vt_index(g):
        bi, si = item(g, 0)
        return bi, 0, 0, si

    pair = lambda n: jax.ShapeDtypeStruct((b, n, s, LANES), jnp.bfloat16)
    tab_spec = pl.BlockSpec((tm, LANES), lambda g: (item(g, 1)[1], 0))
    raw = pltpu.VMEM((tm, A_WIDTH + LANES), jnp.float32)
    return pl.pallas_call(
        _proj_kernel,
        grid=(n_items + 1,),
        in_specs=[
            pl.BlockSpec((1, tm, d), x_index),
            _const_spec((1, d)),
            _const_spec((d, n_cols)),
            _const_spec((1, LANES)),
            _const_spec((1, LANES)),
            tab_spec, tab_spec, tab_spec,
        ],
        out_specs=[pair_spec(N_PAIRS, 1), pair_spec(1, 1),
                   pl.BlockSpec((1, 1, LANES, tm), vt_index),
                   pair_spec(N_PAIRS, 0), pair_spec(N_PAIRS, 0), pair_spec(N_PAIRS, 0)],
        out_shape=[pair(N_PAIRS), pair(1),
                   jax.ShapeDtypeStruct((b, 1, LANES, s), jnp.bfloat16),
                   pair(N_PAIRS), pair(N_PAIRS), pair(N_PAIRS)],
        scratch_shapes=[raw, raw],
        compiler_params=pltpu.CompilerParams(
            dimension_semantics=("arbitrary",), vmem_limit_bytes=VMEM_LIMIT),
        name="in_projection",
    )(x, norm_g, w_ext, q_g, k_g, cos, sa, sb)


def _with_ones(v):
    return jnp.concatenate([v, jnp.ones_like(v)], axis=1)


def _normalise(res):
    return res[:, :LANES] / res[:, LANES:]


def _global_attn_kernel(q_ref, k_ref, qn_ref, kn_ref, vt_ref, o_ref, s_0, s_1, m_0, m_1):
    g = pl.program_id(0)
    n_q = q_ref.shape[2]
    n_tiles = 2 * (n_q // ATTN_TILE)
    lo = lax.broadcasted_iota(jnp.int32, (1, LANES), 1) < HEAD_DIM
    top = lax.broadcasted_iota(jnp.int32, (LANES, 1), 0) < HEAD_DIM
    sub = 8
    bufs = ((s_0, m_0), (s_1, m_1))

    def scores(q_blk, k_blk, t, s_w, m_w):
        rows = slice((t // 2) * ATTN_TILE, (t // 2 + 1) * ATTN_TILE)
        q = q_blk[0, 0, rows, :]
        zero = jnp.zeros_like(q)
        qm = jnp.where(lo, q, zero) if t % 2 == 0 else jnp.where(lo, zero, q)
        half = k_blk.shape[2] // 2
        for keys in (slice(0, half), slice(half, 2 * half)):
            s_w[keys] = lax.dot_general(k_blk[0, 0, keys, :], qm, _NT,
                                        preferred_element_type=jnp.float32)
        for c in range(0, ATTN_TILE, LANES):
            m = jnp.max(s_w[:, c:c + LANES], axis=0, keepdims=True)
            m_w[:, c:c + LANES] = jnp.broadcast_to(m, (sub, LANES))

    @pl.when(g == 0)
    def _():
        scores(q_ref, k_ref, 0, s_0, m_0)

    vt = vt_ref[0, 0]
    one = jnp.ones_like(vt)
    vt_ones = (jnp.where(top, vt, one), jnp.where(top, one, vt))
    n_keys = vt.shape[1]
    o_lo = None
    for t in range(n_tiles):
        s_w, m_w = bufs[(t + 1) % 2]
        if t + 1 < n_tiles:
            scores(q_ref, k_ref, t + 1, s_w, m_w)
        else:
            scores(qn_ref, kn_ref, 0, s_w, m_w)
        s_r, m_r = bufs[t % 2]
        m_all = jnp.tile(m_r[...], (SOFTMAX_ROWS // sub, 1))
        pt = jnp.concatenate(
            [jnp.exp2(s_r[a:a + SOFTMAX_ROWS] - m_all).astype(jnp.bfloat16)
             for a in range(0, n_keys, SOFTMAX_ROWS)], axis=0)
        d = jnp.dot(vt_ones[t % 2], pt, preferred_element_type=jnp.float32)
        if t % 2 == 0:
            o_lo = d[:HEAD_DIM] / d[HEAD_DIM:]
        else:
            ot = jnp.concatenate([o_lo, d[HEAD_DIM:] / d[:HEAD_DIM]], axis=0)
            rows = slice((t // 2) * ATTN_TILE, (t // 2 + 1) * ATTN_TILE)
            o_ref[0, 0, rows, :] = ot.T.astype(o_ref.dtype)


def _global_attention(qa, ka, va):
    b, n_pairs, s, _ = qa.shape
    n_items = b * n_pairs

    def q_map(ahead):
        def index(g):
            i = jnp.minimum(g + ahead, n_items - 1)
            return i // n_pairs, i % n_pairs, 0, 0
        return index

    def kv_map(ahead):
        def index(g):
            return jnp.minimum(g + ahead, n_items - 1) // n_pairs, 0, 0, 0
        return index

    seq = lambda index: pl.BlockSpec((1, 1, s, LANES), index)
    return pl.pallas_call(
        _global_attn_kernel,
        grid=(n_items,),
        in_specs=[seq(q_map(0)), seq(kv_map(0)), seq(q_map(1)), seq(kv_map(1)),
                  pl.BlockSpec((1, 1, LANES, s), kv_map(0))],
        out_specs=seq(q_map(0)),
        out_shape=jax.ShapeDtypeStruct(qa.shape, jnp.bfloat16),
        scratch_shapes=[pltpu.VMEM((s, ATTN_TILE), jnp.float32),
                        pltpu.VMEM((s, ATTN_TILE), jnp.float32),
                        pltpu.VMEM((8, ATTN_TILE), jnp.float32),
                        pltpu.VMEM((8, ATTN_TILE), jnp.float32)],
        compiler_params=pltpu.CompilerParams(
            dimension_semantics=("arbitrary",), vmem_limit_bytes=VMEM_LIMIT),
        name="global_attention",
    )(qa, ka, qa, ka, va)


def _nat_kernel(q_ref, k_ref, v_ref, bias_ref, o_ref, *, rows):
    lo = lax.broadcasted_iota(jnp.int32, (1, LANES), 1) < HEAD_DIM

    def band(r):
        rs = min(max(r - WIN_ROWS // 2, 0), rows - WIN_ROWS)
        return rs, slice(rs * GRID_W, rs * GRID_W + BAND)

    def batch_body(i, carry):
        q = q_ref[i, 0]
        zero = jnp.zeros_like(q)
        q_lo = jnp.where(lo, q, zero)
        q_hi = jnp.where(lo, zero, q)
        v1 = _with_ones(v_ref[i, 0])

        def scores(r):
            rs, keys = band(r)
            tok = slice(r * GRID_W, (r + 1) * GRID_W)
            qs = jnp.concatenate([q_lo[tok], q_hi[tok]], axis=0)
            s = lax.dot_general(qs, k_ref[i, 0, keys, :], _NT,
                                preferred_element_type=jnp.float32)
            return s + bias_ref[0, rs - r + WIN_ROWS - 1]

        s_cur = scores(0)
        for r in range(rows):
            s_nxt = scores(r + 1) if r + 1 < rows else None
            p = jnp.exp2(s_cur - jnp.max(s_cur, axis=-1, keepdims=True))
            o = _normalise(jnp.dot(p.astype(jnp.bfloat16), v1[band(r)[1]],
                                   preferred_element_type=jnp.float32))
            o_ref[i, 0, r * GRID_W:(r + 1) * GRID_W, :] = jnp.where(
                lo, o[:GRID_W], o[GRID_W:]).astype(o_ref.dtype)
            s_cur = s_nxt
        return carry

    lax.fori_loop(0, q_ref.shape[0], batch_body, 0)


def _neighbourhood_attention(qb, kb, vb, bias):
    b, n_pairs, s, _ = qb.shape
    rows = s // GRID_W
    bb = NAT_BATCH
    seq = pl.BlockSpec((bb, 1, s, LANES), lambda p, bi: (bi, p, 0, 0))
    bias_spec = pl.BlockSpec((1, WIN_ROWS, 2 * GRID_W, BAND), lambda p, bi: (p, 0, 0, 0))
    return pl.pallas_call(
        functools.partial(_nat_kernel, rows=rows),
        grid=(n_pairs, b // bb),
        in_specs=[seq, seq, seq, bias_spec],
        out_specs=seq,
        out_shape=jax.ShapeDtypeStruct(qb.shape, jnp.bfloat16),
        compiler_params=pltpu.CompilerParams(
            dimension_semantics=("parallel", "parallel"), vmem_limit_bytes=VMEM_LIMIT),
        name="neighbourhood_attention",
    )(qb, kb, vb, bias)


def _mlp_kernel(x_ref, ya_ref, yb_ref, ga_ref, gb_ref, wo_ref, gm_ref, wu_ref, wd_ref,
                gf_ref, o_ref):
    ya = jnp.concatenate([ya_ref[0, p] for p in range(N_PAIRS)], axis=-1)
    yb = jnp.concatenate([yb_ref[0, p] for p in range(N_PAIRS)], axis=-1)
    y = jnp.concatenate([_rms(ya.astype(jnp.float32), ga_ref[...]),
                         _rms(yb.astype(jnp.float32), gb_ref[...])], axis=-1)
    x1 = x_ref[0] + jnp.dot(y.astype(jnp.bfloat16), wo_ref[...],
                            preferred_element_type=jnp.float32)
    h = _rms(x1, gm_ref[...]).astype(jnp.bfloat16)
    acc = x1
    for c in range(D_FF // FF_CHUNK):
        sl = slice(c * FF_CHUNK, (c + 1) * FF_CHUNK)
        u = jnp.dot(h, wu_ref[:, sl], preferred_element_type=jnp.float32)
        u = jnp.square(jnp.maximum(u, 0.0)).astype(jnp.bfloat16)
        acc = acc + jnp.dot(u, wd_ref[sl, :], preferred_element_type=jnp.float32)
    o_ref[0] = _rms(acc, gf_ref[...])


def _out_mlp(x, ya, yb, ga, gb, w_out, gm, w_up, w_down, gf):
    b, s, d = x.shape
    tm = MLP_TM
    x_spec = pl.BlockSpec((1, tm, d), lambda bi, si: (bi, si, 0))
    y_spec = pl.BlockSpec((1, N_PAIRS, tm, LANES), lambda bi, si: (bi, 0, si, 0))
    return pl.pallas_call(
        _mlp_kernel,
        grid=(b, s // tm),
        in_specs=[x_spec, y_spec, y_spec,
                  _const_spec((1, A_WIDTH)), _const_spec((1, B_WIDTH)),
                  _const_spec(w_out.shape), _const_spec((1, d)),
                  _const_spec(w_up.shape), _const_spec(w_down.shape),
                  _const_spec((1, d))],
        out_specs=x_spec,
        out_shape=jax.ShapeDtypeStruct(x.shape, jnp.float32),
        compiler_params=pltpu.CompilerParams(
            dimension_semantics=("parallel", "parallel"), vmem_limit_bytes=VMEM_LIMIT),
        name="out_proj_mlp",
    )(x, ya, yb, ga, gb, w_out, gm, w_up, w_down, gf)


def _rope_tables(seq_len):
    t = jnp.arange(seq_len, dtype=jnp.int32)
    freqs = ROPE_THETA ** (-jnp.arange(ROPE_HALF, dtype=jnp.float32) / ROPE_HALF)
    zeros = jnp.zeros((seq_len, ROPE_HALF), jnp.float32)
    cos_parts, sa_parts, sb_parts = [], [], []
    for pos in (t // GRID_W, t % GRID_W):
        ang = pos.astype(jnp.float32)[:, None] * freqs[None, :]
        c, sn = jnp.cos(ang), jnp.sin(ang)
        cos_parts += [c, c]
        sa_parts += [-sn, zeros]
        sb_parts += [zeros, sn]
    head = lambda parts: jnp.tile(jnp.concatenate(parts, axis=-1), (1, LANES // HEAD_DIM))
    return head(cos_parts), head(sa_parts), head(sb_parts)


def _pair_heads(a, axis):
    shape = a.shape
    a = a.reshape(shape[:axis] + (A_KV_HEADS, A_GROUP, HEAD_DIM) + shape[axis + 1:])
    return jnp.swapaxes(a, axis, axis + 1).reshape(shape)


def _paired_w_in(w_in):
    qa = _pair_heads(w_in[:, :A_WIDTH], 1)
    return jnp.concatenate([qa, w_in[:, A_WIDTH:]], axis=1).astype(jnp.bfloat16)


def _encoder(x, p):
    qa, ka, va, qb, kb, vb = _in_projection(
        x, p["norm_attn_g"], p["w_in"], p["q_g"], p["k_g"], *p["rope"])
    ya = _global_attention(qa, ka, va)
    yb = _neighbourhood_attention(qb, kb, vb, p["bias"])
    return _out_mlp(x, ya, yb, p["ga"], p["gb"], p["w_out"], p["gm"], p["w_up"],
                    p["w_down"], p["gf"])


def kernel(x_prompt, x_sample, norm_attn_g, w_in, q_norm_g, k_norm_g, nat_rel_bias,
           out_norm_a_g, out_norm_b_g, w_out, norm_mlp_g, w_up, w_down, final_norm_g):
    assert norm_attn_g.shape[0] == 1, "single trunk layer"
    assert x_prompt.shape[1] == x_sample.shape[1]
    row = lambda g: g.reshape(1, -1).astype(jnp.float32)
    pair_gain = lambda g: jnp.tile(row(g), (1, LANES // HEAD_DIM))
    params = {
        "norm_attn_g": row(norm_attn_g[0]),
        "w_in": _paired_w_in(w_in[0]),
        "q_g": pair_gain(q_norm_g[0]),
        "k_g": pair_gain(k_norm_g[0]),
        "rope": _rope_tables(x_prompt.shape[1]),
        "bias": _bias_tables(nat_rel_bias[0]),
        "ga": row(_pair_heads(out_norm_a_g[0], 0)),
        "gb": row(out_norm_b_g[0]),
        "w_out": jnp.concatenate([_pair_heads(w_out[0, :A_WIDTH], 0), w_out[0, A_WIDTH:]],
                                 axis=0).astype(jnp.bfloat16),
        "gm": row(norm_mlp_g[0]),
        "w_up": w_up[0].astype(jnp.bfloat16),
        "w_down": w_down[0].astype(jnp.bfloat16),
        "gf": row(final_norm_g),
    }
    return (_encoder(x_prompt, params), _encoder(x_sample, params))
```

```python
import functools

import jax
import jax.numpy as jnp
from jax import lax
from jax.experimental import pallas as pl
from jax.experimental.pallas import tpu as pltpu

D_MODEL = 1024
HEAD_DIM = 64
LANES = 128
A_WIDTH = 512
B_WIDTH = 512
A_KV_HEADS = 2
A_GROUP = 4
B_HEADS = 8
GRID_W = 64
WIN_ROWS = 8
WIN_COLS = 16
ROPE_THETA = 10000.0
ROPE_HALF = 16
D_FF = 4 * D_MODEL
EPS = 1e-6
SCALE = HEAD_DIM ** -0.5
MASKED = -1e30

N_PAIRS = A_WIDTH // LANES
BAND = WIN_ROWS * GRID_W
PROJ_TM = 1024
ATTN_TILE = 256
SOFTMAX_ROWS = 16
NAT_BATCH = 8
LOG2E = 1.4426950408889634
MLP_TM = 1024
MLP_SUB = 256
FF_CHUNK = 1024
VMEM_LIMIT = 56 * 1024 * 1024

_NT = (((1,), (1,)), ((), ()))


def _const_spec(shape):
    zeros = (0,) * len(shape)
    return pl.BlockSpec(shape, lambda *_: zeros, pipeline_mode=pl.Buffered(1))


def _rms(x, g):
    return x * lax.rsqrt(jnp.mean(x * x, axis=-1, keepdims=True) + EPS) * g


def _bias_kernel(rb_ref, out_ref):
    h = pl.program_id(0)
    n_dr = 2 * WIN_ROWS - 1
    n_dc = 2 * WIN_COLS - 1
    c = lax.broadcasted_iota(jnp.int32, (GRID_W, LANES), 0)
    lane = lax.broadcasted_iota(jnp.int32, (GRID_W, LANES), 1)
    kc = lane % GRID_W
    first = lane < GRID_W
    d = kc - c + (WIN_COLS - 1)
    cs = jnp.clip(c - WIN_COLS // 2, 0, GRID_W - WIN_COLS)
    valid = (kc >= cs) & (kc < cs + WIN_COLS)
    hits = [d == dd for dd in range(n_dc)]
    base = h * (n_dr * n_dc)
    tiles = []
    for dr in range(n_dr):
        acc = jnp.zeros((GRID_W, LANES), jnp.float32)
        for dd in range(n_dc):
            acc = jnp.where(hits[dd], rb_ref[base + dr * n_dc + dd], acc)
        tiles.append(jnp.where(valid, acc * LOG2E, MASKED))
    pair_tiles = [jnp.where(first, tiles[dr], tiles[dr + 1])
                  for dr in range(n_dr - 1)]
    for a in range(WIN_ROWS):
        for jj in range(WIN_ROWS // 2):
            out_ref[0, a, :, jj * LANES:(jj + 1) * LANES] = pair_tiles[a + 2 * jj]


def _bias_tables(rel_bias):
    flat = rel_bias.reshape(-1).astype(jnp.float32)
    return pl.pallas_call(
        _bias_kernel,
        grid=(B_HEADS,),
        in_specs=[pl.BlockSpec(memory_space=pltpu.SMEM)],
        out_specs=pl.BlockSpec((1, WIN_ROWS, GRID_W, BAND), lambda h: (h // 2, 0, h % 2, 0)),
        out_shape=jax.ShapeDtypeStruct((N_PAIRS, WIN_ROWS, 2 * GRID_W, BAND), jnp.float32),
        name="nat_bias_tables",
    )(flat)


def _proj_kernel(x_ref, g_ref, w_ref, qg_ref, kg_ref, cos_ref, sa_ref, sb_ref,
                 qa_ref, ka_ref, va_ref, qb_ref, kb_ref, vb_ref, raw_a, raw_b):
    g = pl.program_id(0)
    lo = lax.broadcasted_iota(jnp.int32, (1, LANES), 1) < HEAD_DIM
    qk_width = A_WIDTH + LANES

    @pl.when(g == 0)
    def _():
        raw_b[...] = jnp.zeros_like(raw_b)

    def norm_rope(t, gain):
        sq = t * t
        s_lo = jnp.sum(jnp.where(lo, sq, 0.0), axis=-1, keepdims=True)
        s_hi = jnp.sum(jnp.where(lo, 0.0, sq), axis=-1, keepdims=True)
        ms = jnp.where(lo, s_lo, s_hi) * (1.0 / HEAD_DIM)
        y = t * lax.rsqrt(ms + EPS) * gain
        return (y * cos_ref[...] + pltpu.roll(y, LANES - ROPE_HALF, 1) * sa_ref[...]
                + pltpu.roll(y, ROPE_HALF, 1) * sb_ref[...])

    def step(raw_w, raw_r):
        h = _rms(x_ref[0], g_ref[...]).astype(jnp.bfloat16)

        def proj(col, width):
            return jnp.dot(h, w_ref[:, col:col + width], preferred_element_type=jnp.float32)

        raw_w[:, :A_WIDTH] = proj(0, A_WIDTH)
        kva = proj(A_WIDTH, 2 * LANES)
        raw_w[:, A_WIDTH:qk_width] = kva[:, :LANES]
        va_ref[0, 0] = kva[:, LANES:].T.astype(jnp.bfloat16)
        col = A_WIDTH + 2 * LANES
        qb = proj(col, B_WIDTH)
        kb = proj(col + B_WIDTH, B_WIDTH)
        vb = proj(col + 2 * B_WIDTH, B_WIDTH)
        for p in range(N_PAIRS):
            sl = slice(p * LANES, (p + 1) * LANES)
            qb_ref[0, p] = (qb[:, sl] * (SCALE * LOG2E)).astype(jnp.bfloat16)
            kb_ref[0, p] = kb[:, sl].astype(jnp.bfloat16)
            vb_ref[0, p] = vb[:, sl].astype(jnp.bfloat16)

        for p in range(N_PAIRS):
            t = norm_rope(raw_r[:, p * LANES:(p + 1) * LANES], qg_ref[...])
            qa_ref[0, p] = (t * (SCALE * LOG2E)).astype(jnp.bfloat16)
        ka_ref[0, 0] = norm_rope(raw_r[:, A_WIDTH:qk_width], kg_ref[...]).astype(jnp.bfloat16)

    @pl.when(g % 2 == 0)
    def _():
        step(raw_a, raw_b)

    @pl.when(g % 2 == 1)
    def _():
        step(raw_b, raw_a)


def _in_projection(x, norm_g, w_ext, q_g, k_g, cos, sa, sb):
    b, s, d = x.shape
    tm = PROJ_TM
    n_tiles = s // tm
    n_items = b * n_tiles
    n_cols = w_ext.shape[1]

    def item(g, lag):
        i = jnp.clip(g - lag, 0, n_items - 1)
        return i // n_tiles, i % n_tiles

    def pair_spec(n, lag):
        def index(g):
            bi, si = item(g, lag)
            return bi, 0, si, 0
        return pl.BlockSpec((1, n, tm, LANES), index)

    def x_index(g):
        bi, si = item(g, 0)
        return bi, si, 0

    def vt_index(g):
        bi, si = item(g, 0)
        return bi, 0, 0, si

    pair = lambda n: jax.ShapeDtypeStruct((b, n, s, LANES), jnp.bfloat16)
    tab_spec = pl.BlockSpec((tm, LANES), lambda g: (item(g, 1)[1], 0))
    raw = pltpu.VMEM((tm, A_WIDTH + LANES), jnp.float32)
    return pl.pallas_call(
        _proj_kernel,
        grid=(n_items + 1,),
        in_specs=[
            pl.BlockSpec((1, tm, d), x_index),
            _const_spec((1, d)),
            _const_spec((d, n_cols)),
            _const_spec((1, LANES)),
            _const_spec((1, LANES)),
            tab_spec, tab_spec, tab_spec,
        ],
        out_specs=[pair_spec(N_PAIRS, 1), pair_spec(1, 1),
                   pl.BlockSpec((1, 1, LANES, tm), vt_index),
                   pair_spec(N_PAIRS, 0), pair_spec(N_PAIRS, 0), pair_spec(N_PAIRS, 0)],
        out_shape=[pair(N_PAIRS), pair(1),
                   jax.ShapeDtypeStruct((b, 1, LANES, s), jnp.bfloat16),
                   pair(N_PAIRS), pair(N_PAIRS), pair(N_PAIRS)],
        scratch_shapes=[raw, raw],
        compiler_params=pltpu.CompilerParams(
            dimension_semantics=("arbitrary",), vmem_limit_bytes=VMEM_LIMIT),
        name="in_projection",
    )(x, norm_g, w_ext, q_g, k_g, cos, sa, sb)


def _with_ones(v):
    return jnp.concatenate([v, jnp.ones_like(v)], axis=1)


def _normalise(res):
    return res[:, :LANES] / res[:, LANES:]


def _global_attn_kernel(q_ref, k_ref, qn_ref, kn_ref, vt_ref, o_ref, s_0, s_1, m_0, m_1):
    g = pl.program_id(0)
    n_q = q_ref.shape[2]
    n_tiles = 2 * (n_q // ATTN_TILE)
    lo = lax.broadcasted_iota(jnp.int32, (1, LANES), 1) < HEAD_DIM
    top = lax.broadcasted_iota(jnp.int32, (LANES, 1), 0) < HEAD_DIM
    sub = 8
    bufs = ((s_0, m_0), (s_1, m_1))

    def scores(q_blk, k_blk, t, s_w, m_w):
        rows = slice((t // 2) * ATTN_TILE, (t // 2 + 1) * ATTN_TILE)
        q = q_blk[0, 0, rows, :]
        zero = jnp.zeros_like(q)
        qm = jnp.where(lo, q, zero) if t % 2 == 0 else jnp.where(lo, zero, q)
        half = k_blk.shape[2] // 2
        for keys in (slice(0, half), slice(half, 2 * half)):
            s_w[keys] = lax.dot_general(k_blk[0, 0, keys, :], qm, _NT,
                                        preferred_element_type=jnp.float32)
        for c in range(0, ATTN_TILE, LANES):
            m = jnp.max(s_w[:, c:c + LANES], axis=0, keepdims=True)
            m_w[:, c:c + LANES] = jnp.broadcast_to(m, (sub, LANES))

    @pl.when(g == 0)
    def _():
        scores(q_ref, k_ref, 0, s_0, m_0)

    vt = vt_ref[0, 0]
    one = jnp.ones_like(vt)
    vt_ones = (jnp.where(top, vt, one), jnp.where(top, one, vt))
    n_keys = vt.shape[1]
    o_lo = None
    for t in range(n_tiles):
        s_w, m_w = bufs[(t + 1) % 2]
        if t + 1 < n_tiles:
            scores(q_ref, k_ref, t + 1, s_w, m_w)
        else:
            scores(qn_ref, kn_ref, 0, s_w, m_w)
        s_r, m_r = bufs[t % 2]
        m_all = jnp.tile(m_r[...], (SOFTMAX_ROWS // sub, 1))
        pt = jnp.concatenate(
            [jnp.exp2(s_r[a:a + SOFTMAX_ROWS] - m_all).astype(jnp.bfloat16)
             for a in range(0, n_keys, SOFTMAX_ROWS)], axis=0)
        d = jnp.dot(vt_ones[t % 2], pt, preferred_element_type=jnp.float32)
        if t % 2 == 0:
            o_lo = d[:HEAD_DIM] / d[HEAD_DIM:]
        else:
            ot = jnp.concatenate([o_lo, d[HEAD_DIM:] / d[:HEAD_DIM]], axis=0)
            rows = slice((t // 2) * ATTN_TILE, (t // 2 + 1) * ATTN_TILE)
            o_ref[0, 0, rows, :] = ot.T.astype(o_ref.dtype)


def _global_attention(qa, ka, va):
    b, n_pairs, s, _ = qa.shape
    n_items = b * n_pairs

    def q_map(ahead):
        def index(g):
            i = jnp.minimum(g + ahead, n_items - 1)
            return i // n_pairs, i % n_pairs, 0, 0
        return index

    def kv_map(ahead):
        def index(g):
            return jnp.minimum(g + ahead, n_items - 1) // n_pairs, 0, 0, 0
        return index

    seq = lambda index: pl.BlockSpec((1, 1, s, LANES), index)
    return pl.pallas_call(
        _global_attn_kernel,
        grid=(n_items,),
        in_specs=[seq(q_map(0)), seq(kv_map(0)), seq(q_map(1)), seq(kv_map(1)),
                  pl.BlockSpec((1, 1, LANES, s), kv_map(0))],
        out_specs=seq(q_map(0)),
        out_shape=jax.ShapeDtypeStruct(qa.shape, jnp.bfloat16),
        scratch_shapes=[pltpu.VMEM((s, ATTN_TILE), jnp.float32),
                        pltpu.VMEM((s, ATTN_TILE), jnp.float32),
                        pltpu.VMEM((8, ATTN_TILE), jnp.float32),
                        pltpu.VMEM((8, ATTN_TILE), jnp.float32)],
        compiler_params=pltpu.CompilerParams(
            dimension_semantics=("arbitrary",), vmem_limit_bytes=VMEM_LIMIT),
        name="global_attention",
    )(qa, ka, qa, ka, va)


def _nat_kernel(q_ref, k_ref, v_ref, bias_ref, o_ref, *, rows):
    lo = lax.broadcasted_iota(jnp.int32, (1, LANES), 1) < HEAD_DIM

    def band(r):
        rs = min(max(r - WIN_ROWS // 2, 0), rows - WIN_ROWS)
        return rs, slice(rs * GRID_W, rs * GRID_W + BAND)

    def batch_body(i, carry):
        q = q_ref[i, 0]
        zero = jnp.zeros_like(q)
        q_lo = jnp.where(lo, q, zero)
        q_hi = jnp.where(lo, zero, q)
        v1 = _with_ones(v_ref[i, 0])

        def scores(r):
            rs, keys = band(r)
            tok = slice(r * GRID_W, (r + 1) * GRID_W)
            qs = jnp.concatenate([q_lo[tok], q_hi[tok]], axis=0)
            s = lax.dot_general(qs, k_ref[i, 0, keys, :], _NT,
                                preferred_element_type=jnp.float32)
            return s + bias_ref[0, rs - r + WIN_ROWS - 1]

        s_cur = scores(0)
        for r in range(rows):
            s_nxt = scores(r + 1) if r + 1 < rows else None
            p = jnp.exp2(s_cur - jnp.max(s_cur, axis=-1, keepdims=True))
            o = _normalise(jnp.dot(p.astype(jnp.bfloat16), v1[band(r)[1]],
                                   preferred_element_type=jnp.float32))
            o_ref[i, 0, r * GRID_W:(r + 1) * GRID_W, :] = jnp.where(
                lo, o[:GRID_W], o[GRID_W:]).astype(o_ref.dtype)
            s_cur = s_nxt
        return carry

    lax.fori_loop(0, q_ref.shape[0], batch_body, 0)


def _neighbourhood_attention(qb, kb, vb, bias):
    b, n_pairs, s, _ = qb.shape
    rows = s // GRID_W
    bb = NAT_BATCH
    seq = pl.BlockSpec((bb, 1, s, LANES), lambda p, bi: (bi, p, 0, 0))
    bias_spec = pl.BlockSpec((1, WIN_ROWS, 2 * GRID_W, BAND), lambda p, bi: (p, 0, 0, 0))
    return pl.pallas_call(
        functools.partial(_nat_kernel, rows=rows),
        grid=(n_pairs, b // bb),
        in_specs=[seq, seq, seq, bias_spec],
        out_specs=seq,
        out_shape=jax.ShapeDtypeStruct(qb.shape, jnp.bfloat16),
        compiler_params=pltpu.CompilerParams(
            dimension_semantics=("parallel", "parallel"), vmem_limit_bytes=VMEM_LIMIT),
        name="neighbourhood_attention",
    )(qb, kb, vb, bias)


def _mlp_kernel(x_ref, ya_ref, yb_ref, ga_ref, gb_ref, wo_ref, gm_ref, wu_ref, wd_ref,
                gf_ref, o_ref):
    tm = x_ref.shape[1]

    def merge(j):
        rows = slice(j * MLP_SUB, (j + 1) * MLP_SUB)
        ya = jnp.concatenate([ya_ref[0, p, rows, :] for p in range(N_PAIRS)], axis=-1)
        yb = jnp.concatenate([yb_ref[0, p, rows, :] for p in range(N_PAIRS)], axis=-1)
        y = jnp.concatenate([_rms(ya.astype(jnp.float32), ga_ref[...]),
                             _rms(yb.astype(jnp.float32), gb_ref[...])], axis=-1)
        x1 = x_ref[0, rows, :] + jnp.dot(y.astype(jnp.bfloat16), wo_ref[...],
                                         preferred_element_type=jnp.float32)
        return x1, _rms(x1, gm_ref[...]).astype(jnp.bfloat16)

    def mlp(j, x1, h):
        acc = x1
        for c in range(D_FF // FF_CHUNK):
            sl = slice(c * FF_CHUNK, (c + 1) * FF_CHUNK)
            u = jnp.dot(h, wu_ref[:, sl], preferred_element_type=jnp.float32)
            u = jnp.square(jnp.maximum(u, 0.0)).astype(jnp.bfloat16)
            acc = acc + jnp.dot(u, wd_ref[sl, :], preferred_element_type=jnp.float32)
        o_ref[0, j * MLP_SUB:(j + 1) * MLP_SUB, :] = _rms(acc, gf_ref[...])

    n_sub = tm // MLP_SUB
    state = merge(0)
    for j in range(n_sub):
        nxt = merge(j + 1) if j + 1 < n_sub else None
        mlp(j, *state)
        state = nxt


def _out_mlp(x, ya, yb, ga, gb, w_out, gm, w_up, w_down, gf):
    b, s, d = x.shape
    tm = MLP_TM
    x_spec = pl.BlockSpec((1, tm, d), lambda bi, si: (bi, si, 0))
    y_spec = pl.BlockSpec((1, N_PAIRS, tm, LANES), lambda bi, si: (bi, 0, si, 0))
    return pl.pallas_call(
        _mlp_kernel,
        grid=(b, s // tm),
        in_specs=[x_spec, y_spec, y_spec,
                  _const_spec((1, A_WIDTH)), _const_spec((1, B_WIDTH)),
                  _const_spec(w_out.shape), _const_spec((1, d)),
                  _const_spec(w_up.shape), _const_spec(w_down.shape),
                  _const_spec((1, d))],
        out_specs=x_spec,
        out_shape=jax.ShapeDtypeStruct(x.shape, jnp.float32),
        compiler_params=pltpu.CompilerParams(
            dimension_semantics=("parallel", "parallel"), vmem_limit_bytes=VMEM_LIMIT),
        name="out_proj_mlp",
    )(x, ya, yb, ga, gb, w_out, gm, w_up, w_down, gf)


def _rope_tables(seq_len):
    t = jnp.arange(seq_len, dtype=jnp.int32)
    freqs = ROPE_THETA ** (-jnp.arange(ROPE_HALF, dtype=jnp.float32) / ROPE_HALF)
    zeros = jnp.zeros((seq_len, ROPE_HALF), jnp.float32)
    cos_parts, sa_parts, sb_parts = [], [], []
    for pos in (t // GRID_W, t % GRID_W):
        ang = pos.astype(jnp.float32)[:, None] * freqs[None, :]
        c, sn = jnp.cos(ang), jnp.sin(ang)
        cos_parts += [c, c]
        sa_parts += [-sn, zeros]
        sb_parts += [zeros, sn]
    head = lambda parts: jnp.tile(jnp.concatenate(parts, axis=-1), (1, LANES // HEAD_DIM))
    return head(cos_parts), head(sa_parts), head(sb_parts)


def _pair_heads(a, axis):
    shape = a.shape
    a = a.reshape(shape[:axis] + (A_KV_HEADS, A_GROUP, HEAD_DIM) + shape[axis + 1:])
    return jnp.swapaxes(a, axis, axis + 1).reshape(shape)


def _paired_w_in(w_in):
    qa = _pair_heads(w_in[:, :A_WIDTH], 1)
    return jnp.concatenate([qa, w_in[:, A_WIDTH:]], axis=1).astype(jnp.bfloat16)


def _encoder(x, p):
    qa, ka, va, qb, kb, vb = _in_projection(
        x, p["norm_attn_g"], p["w_in"], p["q_g"], p["k_g"], *p["rope"])
    ya = _global_attention(qa, ka, va)
    yb = _neighbourhood_attention(qb, kb, vb, p["bias"])
    return _out_mlp(x, ya, yb, p["ga"], p["gb"], p["w_out"], p["gm"], p["w_up"],
                    p["w_down"], p["gf"])


def kernel(x_prompt, x_sample, norm_attn_g, w_in, q_norm_g, k_norm_g, nat_rel_bias,
           out_norm_a_g, out_norm_b_g, w_out, norm_mlp_g, w_up, w_down, final_norm_g):
    assert norm_attn_g.shape[0] == 1, "single trunk layer"
    assert x_prompt.shape[1] == x_sample.shape[1]
    row = lambda g: g.reshape(1, -1).astype(jnp.float32)
    pair_gain = lambda g: jnp.tile(row(g), (1, LANES // HEAD_DIM))
    params = {
        "norm_attn_g": row(norm_attn_g[0]),
        "w_in": _paired_w_in(w_in[0]),
        "q_g": pair_gain(q_norm_g[0]),
        "k_g": pair_gain(k_norm_g[0]),
        "rope": _rope_tables(x_prompt.shape[1]),
        "bias": _bias_tables(nat_rel_bias[0]),
        "ga": row(_pair_heads(out_norm_a_g[0], 0)),
        "gb": row(out_norm_b_g[0]),
        "w_out": jnp.concatenate([_pair_heads(w_out[0, :A_WIDTH], 0), w_out[0, A_WIDTH:]],
                                 axis=0).astype(jnp.bfloat16),
        "gm": row(norm_mlp_g[0]),
        "w_up": w_up[0].astype(jnp.bfloat16),
        "w_down": w_down[0].astype(jnp.bfloat16),
        "gf": row(final_norm_g),
    }
    return (_encoder(x_prompt, params), _encoder(x_sample, params))
```

```python
import functools

import jax
import jax.numpy as jnp
from jax import lax
from jax.experimental import pallas as pl
from jax.experimental.pallas import tpu as pltpu

D_MODEL = 1024
HEAD_DIM = 64
LANES = 128
A_WIDTH = 512
B_WIDTH = 512
A_KV_HEADS = 2
A_GROUP = 4
B_HEADS = 8
GRID_W = 64
WIN_ROWS = 8
WIN_COLS = 16
ROPE_THETA = 10000.0
ROPE_HALF = 16
D_FF = 4 * D_MODEL
EPS = 1e-6
SCALE = HEAD_DIM ** -0.5
MASKED = -1e30

N_PAIRS = A_WIDTH // LANES
BAND = WIN_ROWS * GRID_W
PROJ_TM = 1024
ATTN_TILE = 256
ATTN_PAIRS = 2
SOFTMAX_ROWS = 16
NAT_BATCH = 4
LOG2E = 1.4426950408889634
MLP_TM = 1024
MLP_SUB = 256
FF_CHUNK = 1024
VMEM_LIMIT = 56 * 1024 * 1024

_NT = (((1,), (1,)), ((), ()))


def _const_spec(shape):
    zeros = (0,) * len(shape)
    return pl.BlockSpec(shape, lambda *_: zeros, pipeline_mode=pl.Buffered(1))


def _rms(x, g):
    return x * lax.rsqrt(jnp.mean(x * x, axis=-1, keepdims=True) + EPS) * g


def _bias_kernel(rb_ref, out_ref):
    h = pl.program_id(0)
    n_dr = 2 * WIN_ROWS - 1
    n_dc = 2 * WIN_COLS - 1
    c = lax.broadcasted_iota(jnp.int32, (GRID_W, LANES), 0)
    lane = lax.broadcasted_iota(jnp.int32, (GRID_W, LANES), 1)
    kc = lane % GRID_W
    first = lane < GRID_W
    d = kc - c + (WIN_COLS - 1)
    cs = jnp.clip(c - WIN_COLS // 2, 0, GRID_W - WIN_COLS)
    valid = (kc >= cs) & (kc < cs + WIN_COLS)
    hits = [d == dd for dd in range(n_dc)]
    base = h * (n_dr * n_dc)
    tiles = []
    for dr in range(n_dr):
        acc = jnp.zeros((GRID_W, LANES), jnp.float32)
        for dd in range(n_dc):
            acc = jnp.where(hits[dd], rb_ref[base + dr * n_dc + dd], acc)
        tiles.append(jnp.where(valid, acc * LOG2E, MASKED))
    pair_tiles = [jnp.where(first, tiles[dr], tiles[dr + 1])
                  for dr in range(n_dr - 1)]
    for a in range(WIN_ROWS):
        for jj in range(WIN_ROWS // 2):
            out_ref[0, a, :, jj * LANES:(jj + 1) * LANES] = pair_tiles[a + 2 * jj]


def _bias_tables(rel_bias):
    flat = rel_bias.reshape(-1).astype(jnp.float32)
    return pl.pallas_call(
        _bias_kernel,
        grid=(B_HEADS,),
        in_specs=[pl.BlockSpec(memory_space=pltpu.SMEM)],
        out_specs=pl.BlockSpec((1, WIN_ROWS, GRID_W, BAND), lambda h: (h // 2, 0, h % 2, 0)),
        out_shape=jax.ShapeDtypeStruct((N_PAIRS, WIN_ROWS, 2 * GRID_W, BAND), jnp.float32),
        name="nat_bias_tables",
    )(flat)


def _proj_kernel(x_ref, g_ref, w_ref, qg_ref, kg_ref, cos_ref, sa_ref, sb_ref,
                 qa_ref, ka_ref, va_ref, qb_ref, kb_ref, vb_ref, raw_a, raw_b):
    g = pl.program_id(0)
    lo = lax.broadcasted_iota(jnp.int32, (1, LANES), 1) < HEAD_DIM
    qk_width = A_WIDTH + LANES

    @pl.when(g == 0)
    def _():
        raw_b[...] = jnp.zeros_like(raw_b)

    def norm_rope(t, gain):
        sq = t * t
        s_lo = jnp.sum(jnp.where(lo, sq, 0.0), axis=-1, keepdims=True)
        s_hi = jnp.sum(jnp.where(lo, 0.0, sq), axis=-1, keepdims=True)
        ms = jnp.where(lo, s_lo, s_hi) * (1.0 / HEAD_DIM)
        y = t * lax.rsqrt(ms + EPS) * gain
        return (y * cos_ref[...] + pltpu.roll(y, LANES - ROPE_HALF, 1) * sa_ref[...]
                + pltpu.roll(y, ROPE_HALF, 1) * sb_ref[...])

    def step(raw_w, raw_r):
        h = _rms(x_ref[0], g_ref[...]).astype(jnp.bfloat16)

        def proj(col, width):
            return jnp.dot(h, w_ref[:, col:col + width], preferred_element_type=jnp.float32)

        raw_w[:, :A_WIDTH] = proj(0, A_WIDTH)
        kva = proj(A_WIDTH, 2 * LANES)
        raw_w[:, A_WIDTH:qk_width] = kva[:, :LANES]
        va_ref[0, 0] = kva[:, LANES:].T.astype(jnp.bfloat16)
        col = A_WIDTH + 2 * LANES
        qb = proj(col, B_WIDTH)
        kb = proj(col + B_WIDTH, B_WIDTH)
        vb = proj(col + 2 * B_WIDTH, B_WIDTH)
        for p in range(N_PAIRS):
            sl = slice(p * LANES, (p + 1) * LANES)
            qb_ref[0, p] = (qb[:, sl] * (SCALE * LOG2E)).astype(jnp.bfloat16)
            kb_ref[0, p] = kb[:, sl].astype(jnp.bfloat16)
            vb_ref[0, p] = vb[:, sl].astype(jnp.bfloat16)

        for p in range(N_PAIRS):
            t = norm_rope(raw_r[:, p * LANES:(p + 1) * LANES], qg_ref[...])
            qa_ref[0, p] = (t * (SCALE * LOG2E)).astype(jnp.bfloat16)
        ka_ref[0, 0] = norm_rope(raw_r[:, A_WIDTH:qk_width], kg_ref[...]).astype(jnp.bfloat16)

    @pl.when(g % 2 == 0)
    def _():
        step(raw_a, raw_b)

    @pl.when(g % 2 == 1)
    def _():
        step(raw_b, raw_a)


def _in_projection(x, norm_g, w_ext, q_g, k_g, cos, sa, sb):
    b, s, d = x.shape
    tm = PROJ_TM
    n_tiles = s // tm
    n_items = b * n_tiles
    n_cols = w_ext.shape[1]

    def item(g, lag):
        i = jnp.clip(g - lag, 0, n_items - 1)
        return i // n_tiles, i % n_tiles

    def pair_spec(n, lag):
        def index(g):
            bi, si = item(g, lag)
            return bi, 0, si, 0
        return pl.BlockSpec((1, n, tm, LANES), index)

    def x_index(g):
        bi, si = item(g, 0)
        return bi, si, 0

    def vt_index(g):
        bi, si = item(g, 0)
        return bi, 0, 0, si

    pair = lambda n: jax.ShapeDtypeStruct((b, n, s, LANES), jnp.bfloat16)
    tab_spec = pl.BlockSpec((tm, LANES), lambda g: (item(g, 1)[1], 0))
    raw = pltpu.VMEM((tm, A_WIDTH + LANES), jnp.float32)
    return pl.pallas_call(
        _proj_kernel,
        grid=(n_items + 1,),
        in_specs=[
            pl.BlockSpec((1, tm, d), x_index),
            _const_spec((1, d)),
            _const_spec((d, n_cols)),
            _const_spec((1, LANES)),
            _const_spec((1, LANES)),
            tab_spec, tab_spec, tab_spec,
        ],
        out_specs=[pair_spec(N_PAIRS, 1), pair_spec(1, 1),
                   pl.BlockSpec((1, 1, LANES, tm), vt_index),
                   pair_spec(N_PAIRS, 0), pair_spec(N_PAIRS, 0), pair_spec(N_PAIRS, 0)],
        out_shape=[pair(N_PAIRS), pair(1),
                   jax.ShapeDtypeStruct((b, 1, LANES, s), jnp.bfloat16),
                   pair(N_PAIRS), pair(N_PAIRS), pair(N_PAIRS)],
        scratch_shapes=[raw, raw],
        compiler_params=pltpu.CompilerParams(
            dimension_semantics=("arbitrary",), vmem_limit_bytes=VMEM_LIMIT),
        name="in_projection",
    )(x, norm_g, w_ext, q_g, k_g, cos, sa, sb)


def _with_ones(v):
    return jnp.concatenate([v, jnp.ones_like(v)], axis=1)


def _normalise(res):
    return res[:, :LANES] / res[:, LANES:]


def _global_attn_kernel(q_ref, k_ref, qn_ref, kn_ref, vt_ref, o_ref, s_0, s_1, m_0, m_1):
    g = pl.program_id(0)
    n_q = q_ref.shape[2]
    per_pair = 2 * (n_q // ATTN_TILE)
    n_tiles = q_ref.shape[1] * per_pair
    lo = lax.broadcasted_iota(jnp.int32, (1, LANES), 1) < HEAD_DIM
    top = lax.broadcasted_iota(jnp.int32, (LANES, 1), 0) < HEAD_DIM
    sub = 8
    bufs = ((s_0, m_0), (s_1, m_1))

    def tile_rows(t):
        j = (t % per_pair) // 2
        return t // per_pair, slice(j * ATTN_TILE, (j + 1) * ATTN_TILE)

    def scores(q_blk, k_blk, t, s_w, m_w):
        pp, rows = tile_rows(t)
        q = q_blk[0, pp, rows, :]
        zero = jnp.zeros_like(q)
        qm = jnp.where(lo, q, zero) if t % 2 == 0 else jnp.where(lo, zero, q)
        half = k_blk.shape[2] // 2
        for keys in (slice(0, half), slice(half, 2 * half)):
            s_w[keys] = lax.dot_general(k_blk[0, 0, keys, :], qm, _NT,
                                        preferred_element_type=jnp.float32)
        for c in range(0, ATTN_TILE, LANES):
            m = jnp.max(s_w[:, c:c + LANES], axis=0, keepdims=True)
            m_w[:, c:c + LANES] = jnp.broadcast_to(m, (sub, LANES))

    @pl.when(g == 0)
    def _():
        scores(q_ref, k_ref, 0, s_0, m_0)

    vt = vt_ref[0, 0]
    one = jnp.ones_like(vt)
    vt_ones = (jnp.where(top, vt, one), jnp.where(top, one, vt))
    n_keys = vt.shape[1]
    o_lo = None
    for t in range(n_tiles):
        s_w, m_w = bufs[(t + 1) % 2]
        if t + 1 < n_tiles:
            scores(q_ref, k_ref, t + 1, s_w, m_w)
        else:
            scores(qn_ref, kn_ref, 0, s_w, m_w)
        s_r, m_r = bufs[t % 2]
        m_all = jnp.tile(m_r[...], (SOFTMAX_ROWS // sub, 1))
        pt = jnp.concatenate(
            [jnp.exp2(s_r[a:a + SOFTMAX_ROWS] - m_all).astype(jnp.bfloat16)
             for a in range(0, n_keys, SOFTMAX_ROWS)], axis=0)
        d = jnp.dot(vt_ones[t % 2], pt, preferred_element_type=jnp.float32)
        if t % 2 == 0:
            o_lo = d[:HEAD_DIM] / d[HEAD_DIM:]
        else:
            ot = jnp.concatenate([o_lo, d[HEAD_DIM:] / d[:HEAD_DIM]], axis=0)
            pp, rows = tile_rows(t)
            o_ref[0, pp, rows, :] = ot.T.astype(o_ref.dtype)


def _global_attention(qa, ka, va):
    b, n_pairs, s, _ = qa.shape
    per_batch = n_pairs // ATTN_PAIRS
    n_items = b * per_batch

    def q_map(ahead):
        def index(g):
            i = jnp.minimum(g + ahead, n_items - 1)
            return i // per_batch, i % per_batch, 0, 0
        return index

    def kv_map(ahead):
        def index(g):
            return jnp.minimum(g + ahead, n_items - 1) // per_batch, 0, 0, 0
        return index

    seq = lambda index: pl.BlockSpec((1, 1, s, LANES), index)
    q_spec = lambda index: pl.BlockSpec((1, ATTN_PAIRS, s, LANES), index)
    return pl.pallas_call(
        _global_attn_kernel,
        grid=(n_items,),
        in_specs=[q_spec(q_map(0)), seq(kv_map(0)), q_spec(q_map(1)), seq(kv_map(1)),
                  pl.BlockSpec((1, 1, LANES, s), kv_map(0))],
        out_specs=q_spec(q_map(0)),
        out_shape=jax.ShapeDtypeStruct(qa.shape, jnp.bfloat16),
        scratch_shapes=[pltpu.VMEM((s, ATTN_TILE), jnp.float32),
                        pltpu.VMEM((s, ATTN_TILE), jnp.float32),
                        pltpu.VMEM((8, ATTN_TILE), jnp.float32),
                        pltpu.VMEM((8, ATTN_TILE), jnp.float32)],
        compiler_params=pltpu.CompilerParams(
            dimension_semantics=("arbitrary",), vmem_limit_bytes=VMEM_LIMIT),
        name="global_attention",
    )(qa, ka, qa, ka, va)


def _nat_kernel(q_ref, k_ref, v_ref, bias_ref, o_ref, *, rows):
    lo = lax.broadcasted_iota(jnp.int32, (1, LANES), 1) < HEAD_DIM

    def band(r):
        rs = min(max(r - WIN_ROWS // 2, 0), rows - WIN_ROWS)
        return rs, slice(rs * GRID_W, rs * GRID_W + BAND)

    def batch_body(i, carry):
        q = q_ref[i, 0]
        zero = jnp.zeros_like(q)
        q_lo = jnp.where(lo, q, zero)
        q_hi = jnp.where(lo, zero, q)
        v1 = _with_ones(v_ref[i, 0])

        def scores(r):
            rs, keys = band(r)
            tok = slice(r * GRID_W, (r + 1) * GRID_W)
            qs = jnp.concatenate([q_lo[tok], q_hi[tok]], axis=0)
            s = lax.dot_general(qs, k_ref[i, 0, keys, :], _NT,
                                preferred_element_type=jnp.float32)
            return s + bias_ref[0, rs - r + WIN_ROWS - 1]

        s_cur = scores(0)
        for r in range(rows):
            s_nxt = scores(r + 1) if r + 1 < rows else None
            p = jnp.exp2(s_cur - jnp.max(s_cur, axis=-1, keepdims=True))
            o = _normalise(jnp.dot(p.astype(jnp.bfloat16), v1[band(r)[1]],
                                   preferred_element_type=jnp.float32))
            o_ref[i, 0, r * GRID_W:(r + 1) * GRID_W, :] = jnp.where(
                lo, o[:GRID_W], o[GRID_W:]).astype(o_ref.dtype)
            s_cur = s_nxt
        return carry

    lax.fori_loop(0, q_ref.shape[0], batch_body, 0)


def _neighbourhood_attention(qb, kb, vb, bias):
    b, n_pairs, s, _ = qb.shape
    rows = s // GRID_W
    bb = NAT_BATCH
    seq = pl.BlockSpec((bb, 1, s, LANES), lambda p, bi: (bi, p, 0, 0))
    bias_spec = pl.BlockSpec((1, WIN_ROWS, 2 * GRID_W, BAND), lambda p, bi: (p, 0, 0, 0))
    return pl.pallas_call(
        functools.partial(_nat_kernel, rows=rows),
        grid=(n_pairs, b // bb),
        in_specs=[seq, seq, seq, bias_spec],
        out_specs=seq,
        out_shape=jax.ShapeDtypeStruct(qb.shape, jnp.bfloat16),
        compiler_params=pltpu.CompilerParams(
            dimension_semantics=("parallel", "parallel"), vmem_limit_bytes=VMEM_LIMIT),
        name="neighbourhood_attention",
    )(qb, kb, vb, bias)


def _mlp_kernel(x_ref, ya_ref, yb_ref, ga_ref, gb_ref, wo_ref, gm_ref, wu_ref, wd_ref,
                gf_ref, o_ref):
    tm = x_ref.shape[1]

    def merge(j):
        rows = slice(j * MLP_SUB, (j + 1) * MLP_SUB)
        ya = jnp.concatenate([ya_ref[0, p, rows, :] for p in range(N_PAIRS)], axis=-1)
        yb = jnp.concatenate([yb_ref[0, p, rows, :] for p in range(N_PAIRS)], axis=-1)
        y = jnp.concatenate([_rms(ya.astype(jnp.float32), ga_ref[...]),
                             _rms(yb.astype(jnp.float32), gb_ref[...])], axis=-1)
        x1 = x_ref[0, rows, :] + jnp.dot(y.astype(jnp.bfloat16), wo_ref[...],
                                         preferred_element_type=jnp.float32)
        return x1, _rms(x1, gm_ref[...]).astype(jnp.bfloat16)

    def mlp(j, x1, h):
        acc = x1
        for c in range(D_FF // FF_CHUNK):
            sl = slice(c * FF_CHUNK, (c + 1) * FF_CHUNK)
            u = jnp.dot(h, wu_ref[:, sl], preferred_element_type=jnp.float32)
            u = jnp.square(jnp.maximum(u, 0.0)).astype(jnp.bfloat16)
            acc = acc + jnp.dot(u, wd_ref[sl, :], preferred_element_type=jnp.float32)
        o_ref[0, j * MLP_SUB:(j + 1) * MLP_SUB, :] = _rms(acc, gf_ref[...])

    n_sub = tm // MLP_SUB
    state = merge(0)
    for j in range(n_sub):
        nxt = merge(j + 1) if j + 1 < n_sub else None
        mlp(j, *state)
        state = nxt


def _out_mlp(x, ya, yb, ga, gb, w_out, gm, w_up, w_down, gf):
    b, s, d = x.shape
    tm = MLP_TM
    x_spec = pl.BlockSpec((1, tm, d), lambda bi, si: (bi, si, 0))
    y_spec = pl.BlockSpec((1, N_PAIRS, tm, LANES), lambda bi, si: (bi, 0, si, 0))
    return pl.pallas_call(
        _mlp_kernel,
        grid=(b, s // tm),
        in_specs=[x_spec, y_spec, y_spec,
                  _const_spec((1, A_WIDTH)), _const_spec((1, B_WIDTH)),
                  _const_spec(w_out.shape), _const_spec((1, d)),
                  _const_spec(w_up.shape), _const_spec(w_down.shape),
                  _const_spec((1, d))],
        out_specs=x_spec,
        out_shape=jax.ShapeDtypeStruct(x.shape, jnp.float32),
        compiler_params=pltpu.CompilerParams(
            dimension_semantics=("parallel", "parallel"), vmem_limit_bytes=VMEM_LIMIT),
        name="out_proj_mlp",
    )(x, ya, yb, ga, gb, w_out, gm, w_up, w_down, gf)


def _rope_tables(seq_len):
    t = jnp.arange(seq_len, dtype=jnp.int32)
    freqs = ROPE_THETA ** (-jnp.arange(ROPE_HALF, dtype=jnp.float32) / ROPE_HALF)
    zeros = jnp.zeros((seq_len, ROPE_HALF), jnp.float32)
    cos_parts, sa_parts, sb_parts = [], [], []
    for pos in (t // GRID_W, t % GRID_W):
        ang = pos.astype(jnp.float32)[:, None] * freqs[None, :]
        c, sn = jnp.cos(ang), jnp.sin(ang)
        cos_parts += [c, c]
        sa_parts += [-sn, zeros]
        sb_parts += [zeros, sn]
    head = lambda parts: jnp.tile(jnp.concatenate(parts, axis=-1), (1, LANES // HEAD_DIM))
    return head(cos_parts), head(sa_parts), head(sb_parts)


def _pair_heads(a, axis):
    shape = a.shape
    a = a.reshape(shape[:axis] + (A_KV_HEADS, A_GROUP, HEAD_DIM) + shape[axis + 1:])
    return jnp.swapaxes(a, axis, axis + 1).reshape(shape)


def _paired_w_in(w_in):
    qa = _pair_heads(w_in[:, :A_WIDTH], 1)
    return jnp.concatenate([qa, w_in[:, A_WIDTH:]], axis=1).astype(jnp.bfloat16)


def _encoder(x, p):
    qa, ka, va, qb, kb, vb = _in_projection(
        x, p["norm_attn_g"], p["w_in"], p["q_g"], p["k_g"], *p["rope"])
    ya = _global_attention(qa, ka, va)
    yb = _neighbourhood_attention(qb, kb, vb, p["bias"])
    return _out_mlp(x, ya, yb, p["ga"], p["gb"], p["w_out"], p["gm"], p["w_up"],
                    p["w_down"], p["gf"])


def kernel(x_prompt, x_sample, norm_attn_g, w_in, q_norm_g, k_norm_g, nat_rel_bias,
           out_norm_a_g, out_norm_b_g, w_out, norm_mlp_g, w_up, w_down, final_norm_g):
    assert norm_attn_g.shape[0] == 1, "single trunk layer"
    assert x_prompt.shape[1] == x_sample.shape[1]
    row = lambda g: g.reshape(1, -1).astype(jnp.float32)
    pair_gain = lambda g: jnp.tile(row(g), (1, LANES // HEAD_DIM))
    params = {
        "norm_attn_g": row(norm_attn_g[0]),
        "w_in": _paired_w_in(w_in[0]),
        "q_g": pair_gain(q_norm_g[0]),
        "k_g": pair_gain(k_norm_g[0]),
        "rope": _rope_tables(x_prompt.shape[1]),
        "bias": _bias_tables(nat_rel_bias[0]),
        "ga": row(_pair_heads(out_norm_a_g[0], 0)),
        "gb": row(out_norm_b_g[0]),
        "w_out": jnp.concatenate([_pair_heads(w_out[0, :A_WIDTH], 0), w_out[0, A_WIDTH:]],
                                 axis=0).astype(jnp.bfloat16),
        "gm": row(norm_mlp_g[0]),
        "w_up": w_up[0].astype(jnp.bfloat16),
        "w_down": w_down[0].astype(jnp.bfloat16),
        "gf": row(final_norm_g),
    }
    return (_encoder(x_prompt, params), _encoder(x_sample, params))
```

```python
import functools

import jax
import jax.numpy as jnp
from jax import lax
from jax.experimental import pallas as pl
from jax.experimental.pallas import tpu as pltpu

D_MODEL = 1024
HEAD_DIM = 64
LANES = 128
A_WIDTH = 512
B_WIDTH = 512
A_KV_HEADS = 2
A_GROUP = 4
B_HEADS = 8
GRID_W = 64
WIN_ROWS = 8
WIN_COLS = 16
ROPE_THETA = 10000.0
ROPE_HALF = 16
D_FF = 4 * D_MODEL
EPS = 1e-6
SCALE = HEAD_DIM ** -0.5
MASKED = -1e30

N_PAIRS = A_WIDTH // LANES
BAND = WIN_ROWS * GRID_W
PROJ_TM = 1024
PROJ_SUB = 256
ATTN_TILE = 256
ATTN_PAIRS = 2
SOFTMAX_ROWS = 16
NAT_BATCH = 4
LOG2E = 1.4426950408889634
MLP_TM = 1024
MLP_SUB = 256
FF_CHUNK = 1024
VMEM_LIMIT = 56 * 1024 * 1024

_NT = (((1,), (1,)), ((), ()))


def _const_spec(shape):
    zeros = (0,) * len(shape)
    return pl.BlockSpec(shape, lambda *_: zeros, pipeline_mode=pl.Buffered(1))


def _rms(x, g):
    return x * lax.rsqrt(jnp.mean(x * x, axis=-1, keepdims=True) + EPS) * g


def _bias_kernel(rb_ref, out_ref):
    h = pl.program_id(0)
    n_dr = 2 * WIN_ROWS - 1
    n_dc = 2 * WIN_COLS - 1
    c = lax.broadcasted_iota(jnp.int32, (GRID_W, LANES), 0)
    lane = lax.broadcasted_iota(jnp.int32, (GRID_W, LANES), 1)
    kc = lane % GRID_W
    first = lane < GRID_W
    d = kc - c + (WIN_COLS - 1)
    cs = jnp.clip(c - WIN_COLS // 2, 0, GRID_W - WIN_COLS)
    valid = (kc >= cs) & (kc < cs + WIN_COLS)
    hits = [d == dd for dd in range(n_dc)]
    base = h * (n_dr * n_dc)
    tiles = []
    for dr in range(n_dr):
        acc = jnp.zeros((GRID_W, LANES), jnp.float32)
        for dd in range(n_dc):
            acc = jnp.where(hits[dd], rb_ref[base + dr * n_dc + dd], acc)
        tiles.append(jnp.where(valid, acc * LOG2E, MASKED))
    pair_tiles = [jnp.where(first, tiles[dr], tiles[dr + 1])
                  for dr in range(n_dr - 1)]
    for a in range(WIN_ROWS):
        for jj in range(WIN_ROWS // 2):
            out_ref[0, a, :, jj * LANES:(jj + 1) * LANES] = pair_tiles[a + 2 * jj]


def _bias_tables(rel_bias):
    flat = rel_bias.reshape(-1).astype(jnp.float32)
    return pl.pallas_call(
        _bias_kernel,
        grid=(B_HEADS,),
        in_specs=[pl.BlockSpec(memory_space=pltpu.SMEM)],
        out_specs=pl.BlockSpec((1, WIN_ROWS, GRID_W, BAND), lambda h: (h // 2, 0, h % 2, 0)),
        out_shape=jax.ShapeDtypeStruct((N_PAIRS, WIN_ROWS, 2 * GRID_W, BAND), jnp.float32),
        name="nat_bias_tables",
    )(flat)


def _proj_kernel(x_ref, g_ref, w_ref, qg_ref, kg_ref, cos_ref, sa_ref, sb_ref,
                 qa_ref, ka_ref, va_ref, qb_ref, kb_ref, vb_ref, raw_a, raw_b):
    g = pl.program_id(0)
    lo = lax.broadcasted_iota(jnp.int32, (1, LANES), 1) < HEAD_DIM
    qk_width = A_WIDTH + LANES

    @pl.when(g == 0)
    def _():
        raw_b[...] = jnp.zeros_like(raw_b)

    def norm_rope(t, gain, rows):
        sq = t * t
        s_lo = jnp.sum(jnp.where(lo, sq, 0.0), axis=-1, keepdims=True)
        s_hi = jnp.sum(jnp.where(lo, 0.0, sq), axis=-1, keepdims=True)
        ms = jnp.where(lo, s_lo, s_hi) * (1.0 / HEAD_DIM)
        y = t * lax.rsqrt(ms + EPS) * gain
        return (y * cos_ref[rows, :] + pltpu.roll(y, LANES - ROPE_HALF, 1) * sa_ref[rows, :]
                + pltpu.roll(y, ROPE_HALF, 1) * sb_ref[rows, :])

    def step(raw_w, raw_r):
        n_sub = x_ref.shape[1] // PROJ_SUB
        sub_rows = lambda j: slice(j * PROJ_SUB, (j + 1) * PROJ_SUB)
        norm = lambda j: _rms(x_ref[0, sub_rows(j), :], g_ref[...]).astype(jnp.bfloat16)
        h_next = norm(0)
        for j in range(n_sub):
            h, rows = h_next, sub_rows(j)
            if j + 1 < n_sub:
                h_next = norm(j + 1)

            def proj(col, width):
                return jnp.dot(h, w_ref[:, col:col + width],
                               preferred_element_type=jnp.float32)

            raw_w[rows, :A_WIDTH] = proj(0, A_WIDTH)
            kva = proj(A_WIDTH, 2 * LANES)
            raw_w[rows, A_WIDTH:qk_width] = kva[:, :LANES]
            va_ref[0, 0, :, rows] = kva[:, LANES:].T.astype(jnp.bfloat16)
            col = A_WIDTH + 2 * LANES
            qb = proj(col, B_WIDTH)
            kb = proj(col + B_WIDTH, B_WIDTH)
            vb = proj(col + 2 * B_WIDTH, B_WIDTH)
            for p in range(N_PAIRS):
                sl = slice(p * LANES, (p + 1) * LANES)
                qb_ref[0, p, rows, :] = (qb[:, sl] * (SCALE * LOG2E)).astype(jnp.bfloat16)
                kb_ref[0, p, rows, :] = kb[:, sl].astype(jnp.bfloat16)
                vb_ref[0, p, rows, :] = vb[:, sl].astype(jnp.bfloat16)

            for p in range(N_PAIRS):
                t = norm_rope(raw_r[rows, p * LANES:(p + 1) * LANES], qg_ref[...], rows)
                qa_ref[0, p, rows, :] = (t * (SCALE * LOG2E)).astype(jnp.bfloat16)
            ka_ref[0, 0, rows, :] = norm_rope(raw_r[rows, A_WIDTH:qk_width], kg_ref[...],
                                              rows).astype(jnp.bfloat16)

    @pl.when(g % 2 == 0)
    def _():
        step(raw_a, raw_b)

    @pl.when(g % 2 == 1)
    def _():
        step(raw_b, raw_a)


def _in_projection(x, norm_g, w_ext, q_g, k_g, cos, sa, sb):
    b, s, d = x.shape
    tm = PROJ_TM
    n_tiles = s // tm
    n_items = b * n_tiles
    n_cols = w_ext.shape[1]

    def item(g, lag):
        i = jnp.clip(g - lag, 0, n_items - 1)
        return i // n_tiles, i % n_tiles

    def pair_spec(n, lag):
        def index(g):
            bi, si = item(g, lag)
            return bi, 0, si, 0
        return pl.BlockSpec((1, n, tm, LANES), index)

    def x_index(g):
        bi, si = item(g, 0)
        return bi, si, 0

    def vt_index(g):
        bi, si = item(g, 0)
        return bi, 0, 0, si

    pair = lambda n: jax.ShapeDtypeStruct((b, n, s, LANES), jnp.bfloat16)
    tab_spec = pl.BlockSpec((tm, LANES), lambda g: (item(g, 1)[1], 0))
    raw = pltpu.VMEM((tm, A_WIDTH + LANES), jnp.float32)
    return pl.pallas_call(
        _proj_kernel,
        grid=(n_items + 1,),
        in_specs=[
            pl.BlockSpec((1, tm, d), x_index),
            _const_spec((1, d)),
            _const_spec((d, n_cols)),
            _const_spec((1, LANES)),
            _const_spec((1, LANES)),
            tab_spec, tab_spec, tab_spec,
        ],
        out_specs=[pair_spec(N_PAIRS, 1), pair_spec(1, 1),
                   pl.BlockSpec((1, 1, LANES, tm), vt_index),
                   pair_spec(N_PAIRS, 0), pair_spec(N_PAIRS, 0), pair_spec(N_PAIRS, 0)],
        out_shape=[pair(N_PAIRS), pair(1),
                   jax.ShapeDtypeStruct((b, 1, LANES, s), jnp.bfloat16),
                   pair(N_PAIRS), pair(N_PAIRS), pair(N_PAIRS)],
        scratch_shapes=[raw, raw],
        compiler_params=pltpu.CompilerParams(
            dimension_semantics=("arbitrary",), vmem_limit_bytes=VMEM_LIMIT),
        name="in_projection",
    )(x, norm_g, w_ext, q_g, k_g, cos, sa, sb)


def _with_ones(v):
    return jnp.concatenate([v, jnp.ones_like(v)], axis=1)


def _normalise(res):
    return res[:, :LANES] / res[:, LANES:]


def _global_attn_kernel(q_ref, k_ref, qn_ref, kn_ref, vt_ref, o_ref, s_0, s_1, m_0, m_1):
    g = pl.program_id(0)
    n_q = q_ref.shape[2]
    per_pair = 2 * (n_q // ATTN_TILE)
    n_tiles = q_ref.shape[1] * per_pair
    lo = lax.broadcasted_iota(jnp.int32, (1, LANES), 1) < HEAD_DIM
    top = lax.broadcasted_iota(jnp.int32, (LANES, 1), 0) < HEAD_DIM
    sub = 8
    bufs = ((s_0, m_0), (s_1, m_1))

    def tile_rows(t):
        j = (t % per_pair) // 2
        return t // per_pair, slice(j * ATTN_TILE, (j + 1) * ATTN_TILE)

    def scores(q_blk, k_blk, t, s_w, m_w):
        pp, rows = tile_rows(t)
        q = q_blk[0, pp, rows, :]
        zero = jnp.zeros_like(q)
        qm = jnp.where(lo, q, zero) if t % 2 == 0 else jnp.where(lo, zero, q)
        half = k_blk.shape[2] // 2
        for keys in (slice(0, half), slice(half, 2 * half)):
            s_w[keys] = lax.dot_general(k_blk[0, 0, keys, :], qm, _NT,
                                        preferred_element_type=jnp.float32)
        for c in range(0, ATTN_TILE, LANES):
            m = jnp.max(s_w[:, c:c + LANES], axis=0, keepdims=True)
            m_w[:, c:c + LANES] = jnp.broadcast_to(m, (sub, LANES))

    @pl.when(g == 0)
    def _():
        scores(q_ref, k_ref, 0, s_0, m_0)

    vt = vt_ref[0, 0]
    one = jnp.ones_like(vt)
    vt_ones = (jnp.where(top, vt, one), jnp.where(top, one, vt))
    n_keys = vt.shape[1]
    o_lo = None
    for t in range(n_tiles):
        s_w, m_w = bufs[(t + 1) % 2]
        if t + 1 < n_tiles:
            scores(q_ref, k_ref, t + 1, s_w, m_w)
        else:
            scores(qn_ref, kn_ref, 0, s_w, m_w)
        s_r, m_r = bufs[t % 2]
        m_all = jnp.tile(m_r[...], (SOFTMAX_ROWS // sub, 1))
        pt = jnp.concatenate(
            [jnp.exp2(s_r[a:a + SOFTMAX_ROWS] - m_all).astype(jnp.bfloat16)
             for a in range(0, n_keys, SOFTMAX_ROWS)], axis=0)
        d = jnp.dot(vt_ones[t % 2], pt, preferred_element_type=jnp.float32)
        if t % 2 == 0:
            o_lo = d[:HEAD_DIM] / d[HEAD_DIM:]
        else:
            ot = jnp.concatenate([o_lo, d[HEAD_DIM:] / d[:HEAD_DIM]], axis=0)
            pp, rows = tile_rows(t)
            o_ref[0, pp, rows, :] = ot.T.astype(o_ref.dtype)


def _global_attention(qa, ka, va):
    b, n_pairs, s, _ = qa.shape
    per_batch = n_pairs // ATTN_PAIRS
    n_items = b * per_batch

    def q_map(ahead):
        def index(g):
            i = jnp.minimum(g + ahead, n_items - 1)
            return i // per_batch, i % per_batch, 0, 0
        return index

    def kv_map(ahead):
        def index(g):
            return jnp.minimum(g + ahead, n_items - 1) // per_batch, 0, 0, 0
        return index

    seq = lambda index: pl.BlockSpec((1, 1, s, LANES), index)
    q_spec = lambda index: pl.BlockSpec((1, ATTN_PAIRS, s, LANES), index)
    return pl.pallas_call(
        _global_attn_kernel,
        grid=(n_items,),
        in_specs=[q_spec(q_map(0)), seq(kv_map(0)), q_spec(q_map(1)), seq(kv_map(1)),
                  pl.BlockSpec((1, 1, LANES, s), kv_map(0))],
        out_specs=q_spec(q_map(0)),
        out_shape=jax.ShapeDtypeStruct(qa.shape, jnp.bfloat16),
        scratch_shapes=[pltpu.VMEM((s, ATTN_TILE), jnp.float32),
                        pltpu.VMEM((s, ATTN_TILE), jnp.float32),
                        pltpu.VMEM((8, ATTN_TILE), jnp.float32),
                        pltpu.VMEM((8, ATTN_TILE), jnp.float32)],
        compiler_params=pltpu.CompilerParams(
            dimension_semantics=("arbitrary",), vmem_limit_bytes=VMEM_LIMIT),
        name="global_attention",
    )(qa, ka, qa, ka, va)


def _nat_kernel(q_ref, k_ref, v_ref, bias_ref, o_ref, *, rows):
    lo = lax.broadcasted_iota(jnp.int32, (1, LANES), 1) < HEAD_DIM

    def band(r):
        rs = min(max(r - WIN_ROWS // 2, 0), rows - WIN_ROWS)
        return rs, slice(rs * GRID_W, rs * GRID_W + BAND)

    def batch_body(i, carry):
        q = q_ref[i, 0]
        zero = jnp.zeros_like(q)
        q_lo = jnp.where(lo, q, zero)
        q_hi = jnp.where(lo, zero, q)
        v1 = _with_ones(v_ref[i, 0])

        def scores(r):
            rs, keys = band(r)
            tok = slice(r * GRID_W, (r + 1) * GRID_W)
            qs = jnp.concatenate([q_lo[tok], q_hi[tok]], axis=0)
            s = lax.dot_general(qs, k_ref[i, 0, keys, :], _NT,
                                preferred_element_type=jnp.float32)
            return s + bias_ref[0, rs - r + WIN_ROWS - 1]

        s_cur = scores(0)
        for r in range(rows):
            s_nxt = scores(r + 1) if r + 1 < rows else None
            p = jnp.exp2(s_cur - jnp.max(s_cur, axis=-1, keepdims=True))
            o = _normalise(jnp.dot(p.astype(jnp.bfloat16), v1[band(r)[1]],
                                   preferred_element_type=jnp.float32))
            o_ref[i, 0, r * GRID_W:(r + 1) * GRID_W, :] = jnp.where(
                lo, o[:GRID_W], o[GRID_W:]).astype(o_ref.dtype)
            s_cur = s_nxt
        return carry

    lax.fori_loop(0, q_ref.shape[0], batch_body, 0)


def _neighbourhood_attention(qb, kb, vb, bias):
    b, n_pairs, s, _ = qb.shape
    rows = s // GRID_W
    bb = NAT_BATCH
    seq = pl.BlockSpec((bb, 1, s, LANES), lambda p, bi: (bi, p, 0, 0))
    bias_spec = pl.BlockSpec((1, WIN_ROWS, 2 * GRID_W, BAND), lambda p, bi: (p, 0, 0, 0))
    return pl.pallas_call(
        functools.partial(_nat_kernel, rows=rows),
        grid=(n_pairs, b // bb),
        in_specs=[seq, seq, seq, bias_spec],
        out_specs=seq,
        out_shape=jax.ShapeDtypeStruct(qb.shape, jnp.bfloat16),
        compiler_params=pltpu.CompilerParams(
            dimension_semantics=("parallel", "parallel"), vmem_limit_bytes=VMEM_LIMIT),
        name="neighbourhood_attention",
    )(qb, kb, vb, bias)


def _mlp_kernel(x_ref, ya_ref, yb_ref, ga_ref, gb_ref, wo_ref, gm_ref, wu_ref, wd_ref,
                gf_ref, o_ref):
    tm = x_ref.shape[1]

    def merge(j):
        rows = slice(j * MLP_SUB, (j + 1) * MLP_SUB)
        ya = jnp.concatenate([ya_ref[0, p, rows, :] for p in range(N_PAIRS)], axis=-1)
        yb = jnp.concatenate([yb_ref[0, p, rows, :] for p in range(N_PAIRS)], axis=-1)
        y = jnp.concatenate([_rms(ya.astype(jnp.float32), ga_ref[...]),
                             _rms(yb.astype(jnp.float32), gb_ref[...])], axis=-1)
        x1 = x_ref[0, rows, :] + jnp.dot(y.astype(jnp.bfloat16), wo_ref[...],
                                         preferred_element_type=jnp.float32)
        return x1, _rms(x1, gm_ref[...]).astype(jnp.bfloat16)

    def mlp(j, x1, h):
        acc = x1
        for c in range(D_FF // FF_CHUNK):
            sl = slice(c * FF_CHUNK, (c + 1) * FF_CHUNK)
            u = jnp.dot(h, wu_ref[:, sl], preferred_element_type=jnp.float32)
            u = jnp.square(jnp.maximum(u, 0.0)).astype(jnp.bfloat16)
            acc = acc + jnp.dot(u, wd_ref[sl, :], preferred_element_type=jnp.float32)
        o_ref[0, j * MLP_SUB:(j + 1) * MLP_SUB, :] = _rms(acc, gf_ref[...])

    n_sub = tm // MLP_SUB
    state = merge(0)
    for j in range(n_sub):
        nxt = merge(j + 1) if j + 1 < n_sub else None
        mlp(j, *state)
        state = nxt


def _out_mlp(x, ya, yb, ga, gb, w_out, gm, w_up, w_down, gf):
    b, s, d = x.shape
    tm = MLP_TM
    x_spec = pl.BlockSpec((1, tm, d), lambda bi, si: (bi, si, 0))
    y_spec = pl.BlockSpec((1, N_PAIRS, tm, LANES), lambda bi, si: (bi, 0, si, 0))
    return pl.pallas_call(
        _mlp_kernel,
        grid=(b, s // tm),
        in_specs=[x_spec, y_spec, y_spec,
                  _const_spec((1, A_WIDTH)), _const_spec((1, B_WIDTH)),
                  _const_spec(w_out.shape), _const_spec((1, d)),
                  _const_spec(w_up.shape), _const_spec(w_down.shape),
                  _const_spec((1, d))],
        out_specs=x_spec,
        out_shape=jax.ShapeDtypeStruct(x.shape, jnp.float32),
        compiler_params=pltpu.CompilerParams(
            dimension_semantics=("parallel", "parallel"), vmem_limit_bytes=VMEM_LIMIT),
        name="out_proj_mlp",
    )(x, ya, yb, ga, gb, w_out, gm, w_up, w_down, gf)


def _rope_tables(seq_len):
    t = jnp.arange(seq_len, dtype=jnp.int32)
    freqs = ROPE_THETA ** (-jnp.arange(ROPE_HALF, dtype=jnp.float32) / ROPE_HALF)
    zeros = jnp.zeros((seq_len, ROPE_HALF), jnp.float32)
    cos_parts, sa_parts, sb_parts = [], [], []
    for pos in (t // GRID_W, t % GRID_W):
        ang = pos.astype(jnp.float32)[:, None] * freqs[None, :]
        c, sn = jnp.cos(ang), jnp.sin(ang)
        cos_parts += [c, c]
        sa_parts += [-sn, zeros]
        sb_parts += [zeros, sn]
    head = lambda parts: jnp.tile(jnp.concatenate(parts, axis=-1), (1, LANES // HEAD_DIM))
    return head(cos_parts), head(sa_parts), head(sb_parts)


def _pair_heads(a, axis):
    shape = a.shape
    a = a.reshape(shape[:axis] + (A_KV_HEADS, A_GROUP, HEAD_DIM) + shape[axis + 1:])
    return jnp.swapaxes(a, axis, axis + 1).reshape(shape)


def _paired_w_in(w_in):
    qa = _pair_heads(w_in[:, :A_WIDTH], 1)
    return jnp.concatenate([qa, w_in[:, A_WIDTH:]], axis=1).astype(jnp.bfloat16)


def _encoder(x, p):
    qa, ka, va, qb, kb, vb = _in_projection(
        x, p["norm_attn_g"], p["w_in"], p["q_g"], p["k_g"], *p["rope"])
    ya = _global_attention(qa, ka, va)
    yb = _neighbourhood_attention(qb, kb, vb, p["bias"])
    return _out_mlp(x, ya, yb, p["ga"], p["gb"], p["w_out"], p["gm"], p["w_up"],
                    p["w_down"], p["gf"])


def kernel(x_prompt, x_sample, norm_attn_g, w_in, q_norm_g, k_norm_g, nat_rel_bias,
           out_norm_a_g, out_norm_b_g, w_out, norm_mlp_g, w_up, w_down, final_norm_g):
    assert norm_attn_g.shape[0] == 1, "single trunk layer"
    assert x_prompt.shape[1] == x_sample.shape[1]
    row = lambda g: g.reshape(1, -1).astype(jnp.float32)
    pair_gain = lambda g: jnp.tile(row(g), (1, LANES // HEAD_DIM))
    params = {
        "norm_attn_g": row(norm_attn_g[0]),
        "w_in": _paired_w_in(w_in[0]),
        "q_g": pair_gain(q_norm_g[0]),
        "k_g": pair_gain(k_norm_g[0]),
        "rope": _rope_tables(x_prompt.shape[1]),
        "bias": _bias_tables(nat_rel_bias[0]),
        "ga": row(_pair_heads(out_norm_a_g[0], 0)),
        "gb": row(out_norm_b_g[0]),
        "w_out": jnp.concatenate([_pair_heads(w_out[0, :A_WIDTH], 0), w_out[0, A_WIDTH:]],
                                 axis=0).astype(jnp.bfloat16),
        "gm": row(norm_mlp_g[0]),
        "w_up": w_up[0].astype(jnp.bfloat16),
        "w_down": w_down[0].astype(jnp.bfloat16),
        "gf": row(final_norm_g),
    }
    return (_encoder(x_prompt, params), _encoder(x_sample, params))
```

```python
import functools

import jax
import jax.numpy as jnp
import numpy as np
from jax import lax
from jax.experimental import pallas as pl
from jax.experimental.pallas import tpu as pltpu

D_MODEL = 1024
HEAD_DIM = 64
LANES = 128
A_WIDTH = 512
B_WIDTH = 512
A_KV_HEADS = 2
A_GROUP = 4
B_HEADS = 8
GRID_W = 64
WIN_ROWS = 8
WIN_COLS = 16
ROPE_THETA = 10000.0
ROPE_HALF = 16
D_FF = 4 * D_MODEL
EPS = 1e-6
SCALE = HEAD_DIM ** -0.5
MASKED = -1e30

N_PAIRS = A_WIDTH // LANES
BAND = WIN_ROWS * GRID_W
PROJ_TM = 1024
PROJ_SUB = 256
ATTN_TILE = 256
ATTN_PAIRS = 2
SOFTMAX_ROWS = 16
NAT_BATCH = 4
LOG2E = 1.4426950408889634
MLP_TM = 1024
MLP_SUB = 256
FF_CHUNK = 1024
VMEM_LIMIT = 56 * 1024 * 1024

_NT = (((1,), (1,)), ((), ()))


def _const_spec(shape):
    zeros = (0,) * len(shape)
    return pl.BlockSpec(shape, lambda *_: zeros, pipeline_mode=pl.Buffered(1))


def _rms(x, g):
    return x * lax.rsqrt(jnp.mean(x * x, axis=-1, keepdims=True) + EPS) * g


def _bias_kernel(rb_ref, out_ref):
    h = pl.program_id(0)
    n_dr = 2 * WIN_ROWS - 1
    n_dc = 2 * WIN_COLS - 1
    c = lax.broadcasted_iota(jnp.int32, (GRID_W, LANES), 0)
    lane = lax.broadcasted_iota(jnp.int32, (GRID_W, LANES), 1)
    kc = lane % GRID_W
    first = lane < GRID_W
    d = kc - c + (WIN_COLS - 1)
    cs = jnp.clip(c - WIN_COLS // 2, 0, GRID_W - WIN_COLS)
    valid = (kc >= cs) & (kc < cs + WIN_COLS)
    hits = [d == dd for dd in range(n_dc)]
    base = h * (n_dr * n_dc)
    tiles = []
    for dr in range(n_dr):
        acc = jnp.zeros((GRID_W, LANES), jnp.float32)
        for dd in range(n_dc):
            acc = jnp.where(hits[dd], rb_ref[base + dr * n_dc + dd], acc)
        tiles.append(jnp.where(valid, acc * LOG2E, MASKED))
    pair_tiles = [jnp.where(first, tiles[dr], tiles[dr + 1])
                  for dr in range(n_dr - 1)]
    for a in range(WIN_ROWS):
        for jj in range(WIN_ROWS // 2):
            out_ref[0, a, :, jj * LANES:(jj + 1) * LANES] = pair_tiles[a + 2 * jj]


def _bias_tables(rel_bias):
    flat = rel_bias.reshape(-1).astype(jnp.float32)
    return pl.pallas_call(
        _bias_kernel,
        grid=(B_HEADS,),
        in_specs=[pl.BlockSpec(memory_space=pltpu.SMEM)],
        out_specs=pl.BlockSpec((1, WIN_ROWS, GRID_W, BAND), lambda h: (h // 2, 0, h % 2, 0)),
        out_shape=jax.ShapeDtypeStruct((N_PAIRS, WIN_ROWS, 2 * GRID_W, BAND), jnp.float32),
        name="nat_bias_tables",
    )(flat)


def _proj_kernel(x_ref, g_ref, w_ref, qg_ref, kg_ref, cos_ref, sa_ref, sb_ref,
                 qa_ref, ka_ref, va_ref, qb_ref, kb_ref, vb_ref, raw_a, raw_b):
    g = pl.program_id(0)
    lo = lax.broadcasted_iota(jnp.int32, (1, LANES), 1) < HEAD_DIM
    qk_width = A_WIDTH + LANES

    @pl.when(g == 0)
    def _():
        raw_b[...] = jnp.zeros_like(raw_b)

    def norm_rope(t, gain, rows):
        sq = t * t
        s_lo = jnp.sum(jnp.where(lo, sq, 0.0), axis=-1, keepdims=True)
        s_hi = jnp.sum(jnp.where(lo, 0.0, sq), axis=-1, keepdims=True)
        ms = jnp.where(lo, s_lo, s_hi) * (1.0 / HEAD_DIM)
        y = t * lax.rsqrt(ms + EPS) * gain
        return (y * cos_ref[rows, :] + pltpu.roll(y, LANES - ROPE_HALF, 1) * sa_ref[rows, :]
                + pltpu.roll(y, ROPE_HALF, 1) * sb_ref[rows, :])

    def step(raw_w, raw_r):
        n_sub = x_ref.shape[1] // PROJ_SUB
        sub_rows = lambda j: slice(j * PROJ_SUB, (j + 1) * PROJ_SUB)
        norm = lambda j: _rms(x_ref[0, sub_rows(j), :], g_ref[...]).astype(jnp.bfloat16)
        h_next = norm(0)
        for j in range(n_sub):
            h, rows = h_next, sub_rows(j)
            if j + 1 < n_sub:
                h_next = norm(j + 1)

            def proj(col, width):
                return jnp.dot(h, w_ref[:, col:col + width],
                               preferred_element_type=jnp.float32)

            raw_w[rows, :A_WIDTH] = proj(0, A_WIDTH)
            kva = proj(A_WIDTH, 2 * LANES)
            raw_w[rows, A_WIDTH:qk_width] = kva[:, :LANES]
            va_ref[0, 0, :, rows] = kva[:, LANES:].T.astype(jnp.bfloat16)
            col = A_WIDTH + 2 * LANES
            qb = proj(col, B_WIDTH)
            kb = proj(col + B_WIDTH, B_WIDTH)
            vb = proj(col + 2 * B_WIDTH, B_WIDTH)
            for p in range(N_PAIRS):
                sl = slice(p * LANES, (p + 1) * LANES)
                qb_ref[0, p, rows, :] = (qb[:, sl] * (SCALE * LOG2E)).astype(jnp.bfloat16)
                kb_ref[0, p, rows, :] = kb[:, sl].astype(jnp.bfloat16)
                vb_ref[0, p, rows, :] = vb[:, sl].astype(jnp.bfloat16)

            for p in range(N_PAIRS):
                t = norm_rope(raw_r[rows, p * LANES:(p + 1) * LANES], qg_ref[...], rows)
                qa_ref[0, p, rows, :] = (t * (SCALE * LOG2E)).astype(jnp.bfloat16)
            ka_ref[0, 0, rows, :] = norm_rope(raw_r[rows, A_WIDTH:qk_width], kg_ref[...],
                                              rows).astype(jnp.bfloat16)

    @pl.when(g % 2 == 0)
    def _():
        step(raw_a, raw_b)

    @pl.when(g % 2 == 1)
    def _():
        step(raw_b, raw_a)


def _in_projection(x, norm_g, w_ext, q_g, k_g, cos, sa, sb):
    b, s, d = x.shape
    tm = PROJ_TM
    n_tiles = s // tm
    n_items = b * n_tiles
    n_cols = w_ext.shape[1]

    def item(g, lag):
        i = jnp.clip(g - lag, 0, n_items - 1)
        return i // n_tiles, i % n_tiles

    def pair_spec(n, lag):
        def index(g):
            bi, si = item(g, lag)
            return bi, 0, si, 0
        return pl.BlockSpec((1, n, tm, LANES), index)

    def x_index(g):
        bi, si = item(g, 0)
        return bi, si, 0

    def vt_index(g):
        bi, si = item(g, 0)
        return bi, 0, 0, si

    pair = lambda n: jax.ShapeDtypeStruct((b, n, s, LANES), jnp.bfloat16)
    tab_spec = pl.BlockSpec((tm, LANES), lambda g: (item(g, 1)[1], 0))
    raw = pltpu.VMEM((tm, A_WIDTH + LANES), jnp.float32)
    return pl.pallas_call(
        _proj_kernel,
        grid=(n_items + 1,),
        in_specs=[
            pl.BlockSpec((1, tm, d), x_index),
            _const_spec((1, d)),
            _const_spec((d, n_cols)),
            _const_spec((1, LANES)),
            _const_spec((1, LANES)),
            tab_spec, tab_spec, tab_spec,
        ],
        out_specs=[pair_spec(N_PAIRS, 1), pair_spec(1, 1),
                   pl.BlockSpec((1, 1, LANES, tm), vt_index),
                   pair_spec(N_PAIRS, 0), pair_spec(N_PAIRS, 0), pair_spec(N_PAIRS, 0)],
        out_shape=[pair(N_PAIRS), pair(1),
                   jax.ShapeDtypeStruct((b, 1, LANES, s), jnp.bfloat16),
                   pair(N_PAIRS), pair(N_PAIRS), pair(N_PAIRS)],
        scratch_shapes=[raw, raw],
        compiler_params=pltpu.CompilerParams(
            dimension_semantics=("arbitrary",), vmem_limit_bytes=VMEM_LIMIT),
        name="in_projection",
    )(x, norm_g, w_ext, q_g, k_g, cos, sa, sb)


def _with_ones(v):
    return jnp.concatenate([v, jnp.ones_like(v)], axis=1)


def _normalise(res):
    return res[:, :LANES] / res[:, LANES:]


def _global_attn_kernel(q_ref, k_ref, qn_ref, kn_ref, vt_ref, o_ref, s_0, s_1, m_0, m_1):
    g = pl.program_id(0)
    n_q = q_ref.shape[2]
    per_pair = 2 * (n_q // ATTN_TILE)
    n_tiles = q_ref.shape[1] * per_pair
    lo = lax.broadcasted_iota(jnp.int32, (1, LANES), 1) < HEAD_DIM
    top = lax.broadcasted_iota(jnp.int32, (LANES, 1), 0) < HEAD_DIM
    sub = 8
    bufs = ((s_0, m_0), (s_1, m_1))

    def tile_rows(t):
        j = (t % per_pair) // 2
        return t // per_pair, slice(j * ATTN_TILE, (j + 1) * ATTN_TILE)

    def scores(q_blk, k_blk, t, s_w, m_w):
        pp, rows = tile_rows(t)
        q = q_blk[0, pp, rows, :]
        zero = jnp.zeros_like(q)
        qm = jnp.where(lo, q, zero) if t % 2 == 0 else jnp.where(lo, zero, q)
        half = k_blk.shape[2] // 2
        for keys in (slice(0, half), slice(half, 2 * half)):
            s_w[keys] = lax.dot_general(k_blk[0, 0, keys, :], qm, _NT,
                                        preferred_element_type=jnp.float32)
        for c in range(0, ATTN_TILE, LANES):
            m = jnp.max(s_w[:, c:c + LANES], axis=0, keepdims=True)
            m_w[:, c:c + LANES] = jnp.broadcast_to(m, (sub, LANES))

    @pl.when(g == 0)
    def _():
        scores(q_ref, k_ref, 0, s_0, m_0)

    vt = vt_ref[0, 0]
    one = jnp.ones_like(vt)
    vt_ones = (jnp.where(top, vt, one), jnp.where(top, one, vt))
    n_keys = vt.shape[1]
    o_lo = None
    for t in range(n_tiles):
        s_w, m_w = bufs[(t + 1) % 2]
        if t + 1 < n_tiles:
            scores(q_ref, k_ref, t + 1, s_w, m_w)
        else:
            scores(qn_ref, kn_ref, 0, s_w, m_w)
        s_r, m_r = bufs[t % 2]
        m_all = jnp.tile(m_r[...], (SOFTMAX_ROWS // sub, 1))
        pt = jnp.concatenate(
            [jnp.exp2(s_r[a:a + SOFTMAX_ROWS] - m_all).astype(jnp.bfloat16)
             for a in range(0, n_keys, SOFTMAX_ROWS)], axis=0)
        d = jnp.dot(vt_ones[t % 2], pt, preferred_element_type=jnp.float32)
        if t % 2 == 0:
            o_lo = d[:HEAD_DIM] / d[HEAD_DIM:]
        else:
            ot = jnp.concatenate([o_lo, d[HEAD_DIM:] / d[:HEAD_DIM]], axis=0)
            pp, rows = tile_rows(t)
            o_ref[0, pp, rows, :] = ot.T.astype(o_ref.dtype)


def _global_attention(qa, ka, va):
    b, n_pairs, s, _ = qa.shape
    per_batch = n_pairs // ATTN_PAIRS
    n_items = b * per_batch

    def q_map(ahead):
        def index(g):
            i = jnp.minimum(g + ahead, n_items - 1)
            return i // per_batch, i % per_batch, 0, 0
        return index

    def kv_map(ahead):
        def index(g):
            return jnp.minimum(g + ahead, n_items - 1) // per_batch, 0, 0, 0
        return index

    seq = lambda index: pl.BlockSpec((1, 1, s, LANES), index)
    q_spec = lambda index: pl.BlockSpec((1, ATTN_PAIRS, s, LANES), index)
    return pl.pallas_call(
        _global_attn_kernel,
        grid=(n_items,),
        in_specs=[q_spec(q_map(0)), seq(kv_map(0)), q_spec(q_map(1)), seq(kv_map(1)),
                  pl.BlockSpec((1, 1, LANES, s), kv_map(0))],
        out_specs=q_spec(q_map(0)),
        out_shape=jax.ShapeDtypeStruct(qa.shape, jnp.bfloat16),
        scratch_shapes=[pltpu.VMEM((s, ATTN_TILE), jnp.float32),
                        pltpu.VMEM((s, ATTN_TILE), jnp.float32),
                        pltpu.VMEM((8, ATTN_TILE), jnp.float32),
                        pltpu.VMEM((8, ATTN_TILE), jnp.float32)],
        compiler_params=pltpu.CompilerParams(
            dimension_semantics=("arbitrary",), vmem_limit_bytes=VMEM_LIMIT),
        name="global_attention",
    )(qa, ka, qa, ka, va)


def _nat_kernel(q_ref, k_ref, v_ref, bias_ref, o_ref, *, rows):
    lo = lax.broadcasted_iota(jnp.int32, (1, LANES), 1) < HEAD_DIM

    def band(r):
        rs = min(max(r - WIN_ROWS // 2, 0), rows - WIN_ROWS)
        return rs, slice(rs * GRID_W, rs * GRID_W + BAND)

    def batch_body(i, carry):
        q = q_ref[i, 0]
        zero = jnp.zeros_like(q)
        q_lo = jnp.where(lo, q, zero)
        q_hi = jnp.where(lo, zero, q)
        v1 = _with_ones(v_ref[i, 0])

        def scores(r):
            rs, keys = band(r)
            tok = slice(r * GRID_W, (r + 1) * GRID_W)
            qs = jnp.concatenate([q_lo[tok], q_hi[tok]], axis=0)
            s = lax.dot_general(qs, k_ref[i, 0, keys, :], _NT,
                                preferred_element_type=jnp.float32)
            return s + bias_ref[0, rs - r + WIN_ROWS - 1]

        s_cur = scores(0)
        for r in range(rows):
            s_nxt = scores(r + 1) if r + 1 < rows else None
            p = jnp.exp2(s_cur - jnp.max(s_cur, axis=-1, keepdims=True))
            o = _normalise(jnp.dot(p.astype(jnp.bfloat16), v1[band(r)[1]],
                                   preferred_element_type=jnp.float32))
            o_ref[i, 0, r * GRID_W:(r + 1) * GRID_W, :] = jnp.where(
                lo, o[:GRID_W], o[GRID_W:]).astype(o_ref.dtype)
            s_cur = s_nxt
        return carry

    lax.fori_loop(0, q_ref.shape[0], batch_body, 0)


def _neighbourhood_attention(qb, kb, vb, bias):
    b, n_pairs, s, _ = qb.shape
    rows = s // GRID_W
    bb = NAT_BATCH
    seq = pl.BlockSpec((bb, 1, s, LANES), lambda p, bi: (bi, p, 0, 0))
    bias_spec = pl.BlockSpec((1, WIN_ROWS, 2 * GRID_W, BAND), lambda p, bi: (p, 0, 0, 0))
    return pl.pallas_call(
        functools.partial(_nat_kernel, rows=rows),
        grid=(n_pairs, b // bb),
        in_specs=[seq, seq, seq, bias_spec],
        out_specs=seq,
        out_shape=jax.ShapeDtypeStruct(qb.shape, jnp.bfloat16),
        compiler_params=pltpu.CompilerParams(
            dimension_semantics=("parallel", "parallel"), vmem_limit_bytes=VMEM_LIMIT),
        name="neighbourhood_attention",
    )(qb, kb, vb, bias)


def _mlp_kernel(x_ref, ya_ref, yb_ref, ga_ref, gb_ref, wo_ref, gm_ref, wu_ref, wd_ref,
                gf_ref, o_ref):
    tm = x_ref.shape[1]

    def merge(j):
        rows = slice(j * MLP_SUB, (j + 1) * MLP_SUB)
        ya = jnp.concatenate([ya_ref[0, p, rows, :] for p in range(N_PAIRS)], axis=-1)
        yb = jnp.concatenate([yb_ref[0, p, rows, :] for p in range(N_PAIRS)], axis=-1)
        y = jnp.concatenate([_rms(ya.astype(jnp.float32), ga_ref[...]),
                             _rms(yb.astype(jnp.float32), gb_ref[...])], axis=-1)
        x1 = x_ref[0, rows, :] + jnp.dot(y.astype(jnp.bfloat16), wo_ref[...],
                                         preferred_element_type=jnp.float32)
        return x1, _rms(x1, gm_ref[...]).astype(jnp.bfloat16)

    def mlp(j, x1, h):
        acc = x1
        for c in range(D_FF // FF_CHUNK):
            sl = slice(c * FF_CHUNK, (c + 1) * FF_CHUNK)
            u = jnp.dot(h, wu_ref[:, sl], preferred_element_type=jnp.float32)
            u = jnp.square(jnp.maximum(u, 0.0)).astype(jnp.bfloat16)
            acc = acc + jnp.dot(u, wd_ref[sl, :], preferred_element_type=jnp.float32)
        o_ref[0, j * MLP_SUB:(j + 1) * MLP_SUB, :] = _rms(acc, gf_ref[...])

    n_sub = tm // MLP_SUB
    state = merge(0)
    for j in range(n_sub):
        nxt = merge(j + 1) if j + 1 < n_sub else None
        mlp(j, *state)
        state = nxt


def _out_mlp(x, ya, yb, ga, gb, w_out, gm, w_up, w_down, gf):
    b, s, d = x.shape
    tm = MLP_TM
    x_spec = pl.BlockSpec((1, tm, d), lambda bi, si: (bi, si, 0))
    y_spec = pl.BlockSpec((1, N_PAIRS, tm, LANES), lambda bi, si: (bi, 0, si, 0))
    return pl.pallas_call(
        _mlp_kernel,
        grid=(b, s // tm),
        in_specs=[x_spec, y_spec, y_spec,
                  _const_spec((1, A_WIDTH)), _const_spec((1, B_WIDTH)),
                  _const_spec(w_out.shape), _const_spec((1, d)),
                  _const_spec(w_up.shape), _const_spec(w_down.shape),
                  _const_spec((1, d))],
        out_specs=x_spec,
        out_shape=jax.ShapeDtypeStruct(x.shape, jnp.float32),
        compiler_params=pltpu.CompilerParams(
            dimension_semantics=("parallel", "parallel"), vmem_limit_bytes=VMEM_LIMIT),
        name="out_proj_mlp",
    )(x, ya, yb, ga, gb, w_out, gm, w_up, w_down, gf)


def _rope_tables(seq_len):
    t = np.arange(seq_len)
    freqs = ROPE_THETA ** (-np.arange(ROPE_HALF, dtype=np.float64) / ROPE_HALF)
    zeros = np.zeros((seq_len, ROPE_HALF))
    cos_parts, sa_parts, sb_parts = [], [], []
    for pos in (t // GRID_W, t % GRID_W):
        ang = pos[:, None] * freqs[None, :]
        c, sn = np.cos(ang), np.sin(ang)
        cos_parts += [c, c]
        sa_parts += [-sn, zeros]
        sb_parts += [zeros, sn]
    head = lambda parts: jnp.asarray(
        np.tile(np.concatenate(parts, axis=-1), (1, LANES // HEAD_DIM)), jnp.float32)
    return head(cos_parts), head(sa_parts), head(sb_parts)


def _pair_heads(a, axis):
    shape = a.shape
    a = a.reshape(shape[:axis] + (A_KV_HEADS, A_GROUP, HEAD_DIM) + shape[axis + 1:])
    return jnp.swapaxes(a, axis, axis + 1).reshape(shape)


def _paired_w_in(w_in):
    qa = _pair_heads(w_in[:, :A_WIDTH], 1)
    return jnp.concatenate([qa, w_in[:, A_WIDTH:]], axis=1).astype(jnp.bfloat16)


def _encoder(x, p):
    qa, ka, va, qb, kb, vb = _in_projection(
        x, p["norm_attn_g"], p["w_in"], p["q_g"], p["k_g"], *p["rope"])
    ya = _global_attention(qa, ka, va)
    yb = _neighbourhood_attention(qb, kb, vb, p["bias"])
    return _out_mlp(x, ya, yb, p["ga"], p["gb"], p["w_out"], p["gm"], p["w_up"],
                    p["w_down"], p["gf"])


def kernel(x_prompt, x_sample, norm_attn_g, w_in, q_norm_g, k_norm_g, nat_rel_bias,
           out_norm_a_g, out_norm_b_g, w_out, norm_mlp_g, w_up, w_down, final_norm_g):
    assert norm_attn_g.shape[0] == 1, "single trunk layer"
    assert x_prompt.shape[1] == x_sample.shape[1]
    row = lambda g: g.reshape(1, -1).astype(jnp.float32)
    pair_gain = lambda g: jnp.tile(row(g), (1, LANES // HEAD_DIM))
    params = {
        "norm_attn_g": row(norm_attn_g[0]),
        "w_in": _paired_w_in(w_in[0]),
        "q_g": pair_gain(q_norm_g[0]),
        "k_g": pair_gain(k_norm_g[0]),
        "rope": _rope_tables(x_prompt.shape[1]),
        "bias": _bias_tables(nat_rel_bias[0]),
        "ga": row(_pair_heads(out_norm_a_g[0], 0)),
        "gb": row(out_norm_b_g[0]),
        "w_out": jnp.concatenate([_pair_heads(w_out[0, :A_WIDTH], 0), w_out[0, A_WIDTH:]],
                                 axis=0).astype(jnp.bfloat16),
        "gm": row(norm_mlp_g[0]),
        "w_up": w_up[0].astype(jnp.bfloat16),
        "w_down": w_down[0].astype(jnp.bfloat16),
        "gf": row(final_norm_g),
    }
    return (_encoder(x_prompt, params), _encoder(x_sample, params))
```
